```python
import math
import jax, jax.numpy as jnp
from jax import lax
import numpy as np

D_MODEL = 1024
BATCH = 8
SEQ = 4096
DEPTH = 2

D_FF = ((8 * D_MODEL // 3 + 127) // 128) * 128
FFN_RES_WEIGHT = 0.5
ALPHA = (2 * DEPTH) ** 0.25
BETA = (8 * DEPTH) ** -0.25
LN_EPS = 1e-5
BLK = 128
CONV_CH = D_MODEL // 4
CONV_WIDTH = 3
DSWA_HEAD_DIM = 64
DSWA_W = D_MODEL - CONV_CH
DSWA_HEADS = DSWA_W // DSWA_HEAD_DIM
DILATIONS = ((128, 1), (512, 4), (2048, 16))
EVEN_PROJ = 3 * CONV_CH + 3 * DSWA_W
MLSTM_HEADS = 4
MLSTM_W = D_MODEL // 2
MLSTM_HEAD_DIM = MLSTM_W // MLSTM_HEADS
MLSTM_CHUNK = 128
SB_HEAD_DIM = 64
SB_W = D_MODEL - MLSTM_W
SB_HEADS = SB_W // SB_HEAD_DIM
ODD_PROJ = 4 * MLSTM_W + 2 * MLSTM_HEADS + 3 * SB_W

kernel_name = "hybrid_shortconv_dilated_mlstm_stickbreaking"


def _split(t, sizes):
    cuts = [int(c) for c in np.cumsum(sizes)[:-1]]
    return jnp.split(t, cuts, axis=-1)


def _post_norm(x, y, g, b):
    z = (ALPHA * x + y).astype(jnp.float32)
    mu = z.mean(-1, keepdims=True)
    var = jnp.square(z - mu).mean(-1, keepdims=True)
    return ((z - mu) * lax.rsqrt(var + LN_EPS) * g + b).astype(x.dtype)


def _swiglu(x, w_in, w_out):
    gate, up = jnp.split(x @ w_in, 2, axis=-1)
    return (jax.nn.silu(gate) * up) @ w_out


def _short_conv(bg, cg, xh, conv_w):
    u = cg * xh
    y = lax.conv_general_dilated(
        u, conv_w[:, None, :], window_strides=(1,),
        padding=((CONV_WIDTH - 1, 0),),
        dimension_numbers=("NWC", "WIO", "NWC"),
        feature_group_count=CONV_CH)
    return bg * y


def _dilated_branch(q, k, v, window, dilation):
    b, s_pad, h, dh = q.shape
    win = window // dilation
    n_sub = s_pad // dilation
    nb = n_sub // BLK

    def to_blocks(t):
        t = t.reshape(b, n_sub, dilation, h, dh).transpose(0, 2, 3, 1, 4)
        return t.reshape(b, dilation, h, nb, BLK, dh)

    def with_prev(t):
        prev = jnp.pad(t[:, :, :, :-1], ((0, 0), (0, 0), (0, 0), (1, 0), (0, 0), (0, 0)))
        return jnp.concatenate([prev, t], axis=-2)

    def to_seq(t):
        t = t.reshape((b, dilation, h, n_sub) + t.shape[5:])
        t = jnp.moveaxis(t, 3, 1)
        return t.reshape((b, s_pad, h) + t.shape[4:])

    qb = to_blocks(q)
    kb = with_prev(to_blocks(k))
    vb = with_prev(to_blocks(v))
    scores = jnp.einsum("bdhnqe,bdhnke->bdhnqk", qb, kb).astype(jnp.float32) / math.sqrt(dh)
    qi = jnp.arange(BLK)[:, None]
    kj = jnp.arange(2 * BLK)[None, :]
    rel = qi + BLK - kj
    band = (rel >= 0) & (rel <= win)
    exists = (jnp.arange(nb)[:, None, None] > 0) | (kj >= BLK)[None]
    scores = jnp.where(band[None] & exists, scores, -jnp.inf)
    m = scores.max(-1)
    p = jnp.exp(scores - m[..., None])
    den = p.sum(-1)
    out = jnp.einsum("bdhnqk,bdhnke->bdhnqe", p, vb.astype(jnp.float32)) / den[..., None]
    return to_seq(out), to_seq(m), to_seq(den)


def _dilated_attention(q, k, v):
    s = q.shape[1]
    unit = max(d for _, d in DILATIONS) * BLK
    s_pad = -(-s // unit) * unit
    pad = ((0, 0), (0, s_pad - s), (0, 0), (0, 0))
    q, k, v = (jnp.pad(t, pad) for t in (q, k, v))
    outs, maxes, dens = zip(*[_dilated_branch(q, k, v, w, d) for w, d in DILATIONS])
    m = jnp.stack(maxes)
    wgt = jnp.stack(dens) * jnp.exp(m - m.max(0))
    out = jnp.einsum("gbsh,gbshe->bshe", wgt, jnp.stack(outs)) / wgt.sum(0)[..., None]
    return out[:, :s]


def _mlstm(q, k, v, i_pre, f_pre):
    b, s, h, dh = q.shape
    L = MLSTM_CHUNK
    nc = s // L
    f32 = jnp.float32

    def chunks(t):
        t = t.astype(f32).reshape((b, nc, L, h) + t.shape[3:])
        return jnp.moveaxis(jnp.moveaxis(t, 1, 0), 3, 2)

    xs = (chunks(q), chunks(k) / math.sqrt(dh), chunks(v),
          chunks(i_pre), chunks(jax.nn.log_sigmoid(f_pre.astype(f32))))
    causal = jnp.tril(jnp.ones((L, L), bool))

    def step(carry, inp):
        C, n, m = carry
        qt, kt, vt, it, lf = inp
        bcum = jnp.cumsum(lf, axis=-1)
        d_intra = jnp.where(causal, bcum[..., :, None] - bcum[..., None, :] + it[..., None, :], -jnp.inf)
        d_inter = bcum + m[..., None]
        m_t = jnp.maximum(d_inter, d_intra.max(-1))
        w_inter = jnp.exp(d_inter - m_t)
        qk = jnp.einsum("bhld,bhsd->bhls", qt, kt) * jnp.exp(d_intra - m_t[..., None])
        num = w_inter[..., None] * jnp.einsum("bhld,bhde->bhle", qt, C) + jnp.einsum("bhls,bhse->bhle", qk, vt)
        den = w_inter * jnp.einsum("bhld,bhd->bhl", qt, n) + qk.sum(-1)
        h_out = num / jnp.maximum(jnp.abs(den), jnp.exp(-m_t))[..., None]
        btot = bcum[..., -1]
        d_state = btot[..., None] - bcum + it
        m_new = jnp.maximum(btot + m, d_state.max(-1))
        w_s = jnp.exp(d_state - m_new[..., None])
        decay = jnp.exp(btot + m - m_new)
        C_new = decay[..., None, None] * C + jnp.einsum("bhs,bhsd,bhse->bhde", w_s, kt, vt)
        n_new = decay[..., None] * n + jnp.einsum("bhs,bhsd->bhd", w_s, kt)
        return (C_new, n_new, m_new), h_out

    init = (jnp.zeros((b, h, dh, dh), f32), jnp.zeros((b, h, dh), f32), jnp.zeros((b, h), f32))
    _, hs = lax.scan(step, init, xs)
    return hs.transpose(1, 0, 3, 2, 4).reshape(b, s, h, dh)


def _head_norm(h, g):
    b, s = h.shape[:2]
    mu = h.mean(-1, keepdims=True)
    var = jnp.square(h - mu).mean(-1, keepdims=True)
    return ((h - mu) * lax.rsqrt(var + LN_EPS)).reshape(b, s, -1) * g


def _stick_breaking(q, k, v):
    b, s, h, dh = q.shape
    nb = s // BLK
    qb = q.reshape(b, nb, BLK, h, dh).transpose(1, 0, 3, 2, 4)
    kt = k.transpose(0, 2, 1, 3)
    vt = v.transpose(0, 2, 1, 3).astype(jnp.float32)
    key_pos = jnp.arange(s)

    def block(args):
        qblk, start = args
        z = jnp.einsum("bhqd,bhkd->bhqk", qblk, kt).astype(jnp.float32) / math.sqrt(dh)
        before = key_pos[None, :] < (start + jnp.arange(BLK))[:, None]
        log_1m = jnp.where(before, jax.nn.log_sigmoid(-z), 0.0)
        suffix = lax.cumsum(log_1m, axis=log_1m.ndim - 1, reverse=True) - log_1m
        a = jnp.where(before, jnp.exp(jax.nn.log_sigmoid(z) + suffix), 0.0)
        return jnp.einsum("bhqk,bhkd->bhqd", a, vt)

    out = lax.map(block, (qb, jnp.arange(nb) * BLK))
    return out.transpose(1, 0, 3, 2, 4).reshape(b, s, h, dh)


def _even_mixer(x, w_in, conv_w, w_out):
    b, s, _ = x.shape
    bg, cg, xh, q, k, v = _split(x @ w_in, [CONV_CH] * 3 + [DSWA_W] * 3)
    y_conv = _short_conv(bg, cg, xh, conv_w)
    heads = lambda t: t.reshape(b, s, DSWA_HEADS, DSWA_HEAD_DIM)
    y_att = _dilated_attention(heads(q), heads(k), heads(v)).reshape(b, s, DSWA_W)
    return jnp.concatenate([y_conv, y_att.astype(x.dtype)], axis=-1) @ w_out


def _odd_mixer(x, w_in, b_i, b_f, norm_g, w_out):
    b, s, _ = x.shape
    q_m, k_m, v_m, o_m, i_g, f_g, q_s, k_s, v_s = _split(
        x @ w_in, [MLSTM_W] * 4 + [MLSTM_HEADS] * 2 + [SB_W] * 3)
    mh = lambda t: t.reshape(b, s, MLSTM_HEADS, MLSTM_HEAD_DIM)
    sh = lambda t: t.reshape(b, s, SB_HEADS, SB_HEAD_DIM)
    h_tilde = _mlstm(mh(q_m), mh(k_m), mh(v_m), i_g + b_i, f_g + b_f)
    h_cell = jax.nn.sigmoid(mh(o_m).astype(jnp.float32)) * h_tilde
    y_m = _head_norm(h_cell, norm_g)
    y_s = _stick_breaking(sh(q_s), sh(k_s), sh(v_s)).reshape(b, s, SB_W)
    return jnp.concatenate([y_m.astype(x.dtype), y_s.astype(x.dtype)], axis=-1) @ w_out


def setup_inputs(seed: int = 0) -> dict:
    key = jax.random.key(seed)
    ks = iter(jax.random.split(key, 40))
    nrm = lambda shape, scale: scale * jax.random.normal(next(ks), shape, jnp.float32)
    gain = lambda: 1.0 + nrm((D_MODEL,), 0.02)
    bias = lambda: nrm((D_MODEL,), 0.02)
    ffn_in = lambda: nrm((D_MODEL, 2 * D_FF), D_MODEL ** -0.5)
    ffn_out = lambda: nrm((D_FF, D_MODEL), BETA * D_FF ** -0.5)
    mix_out = lambda: nrm((D_MODEL, D_MODEL), BETA * D_MODEL ** -0.5)
    return {
        "x": nrm((BATCH, SEQ, D_MODEL), 1.0),
        "l0_ffn1_w_in": ffn_in(), "l0_ffn1_w_out": ffn_out(),
        "l0_ln1_g": gain(), "l0_ln1_b": bias(),
        "l0_mix_w_in": nrm((D_MODEL, EVEN_PROJ), D_MODEL ** -0.5),
        "l0_conv_w": nrm((CONV_WIDTH, CONV_CH), CONV_WIDTH ** -0.5),
        "l0_mix_w_out": mix_out(),
        "l0_ln2_g": gain(), "l0_ln2_b": bias(),
        "l0_ffn2_w_in": ffn_in(), "l0_ffn2_w_out": ffn_out(),
        "l0_ln3_g": gain(), "l0_ln3_b": bias(),
        "l1_ffn1_w_in": ffn_in(), "l1_ffn1_w_out": ffn_out(),
        "l1_ln1_g": gain(), "l1_ln1_b": bias(),
        "l1_mix_w_in": nrm((D_MODEL, ODD_PROJ), D_MODEL ** -0.5),
        "l1_mlstm_b_i": nrm((MLSTM_HEADS,), 0.1),
        "l1_mlstm_b_f": jnp.linspace(3.0, 6.0, MLSTM_HEADS, dtype=jnp.float32) + nrm((MLSTM_HEADS,), 0.1),
        "l1_mlstm_norm_g": 1.0 + nrm((MLSTM_W,), 0.02),
        "l1_mix_w_out": mix_out(),
        "l1_ln2_g": gain(), "l1_ln2_b": bias(),
        "l1_ffn2_w_in": ffn_in(), "l1_ffn2_w_out": ffn_out(),
        "l1_ln3_g": gain(), "l1_ln3_b": bias(),
    }


def reference(x,
              l0_ffn1_w_in, l0_ffn1_w_out, l0_ln1_g, l0_ln1_b,
              l0_mix_w_in, l0_conv_w, l0_mix_w_out, l0_ln2_g, l0_ln2_b,
              l0_ffn2_w_in, l0_ffn2_w_out, l0_ln3_g, l0_ln3_b,
              l1_ffn1_w_in, l1_ffn1_w_out, l1_ln1_g, l1_ln1_b,
              l1_mix_w_in, l1_mlstm_b_i, l1_mlstm_b_f, l1_mlstm_norm_g, l1_mix_w_out,
              l1_ln2_g, l1_ln2_b,
              l1_ffn2_w_in, l1_ffn2_w_out, l1_ln3_g, l1_ln3_b):
    ffn1 = ((l0_ffn1_w_in, l0_ffn1_w_out), (l1_ffn1_w_in, l1_ffn1_w_out))
    ffn2 = ((l0_ffn2_w_in, l0_ffn2_w_out), (l1_ffn2_w_in, l1_ffn2_w_out))
    ln1 = ((l0_ln1_g, l0_ln1_b), (l1_ln1_g, l1_ln1_b))
    ln2 = ((l0_ln2_g, l0_ln2_b), (l1_ln2_g, l1_ln2_b))
    ln3 = ((l0_ln3_g, l0_ln3_b), (l1_ln3_g, l1_ln3_b))
    for layer in range(DEPTH):
        x = _post_norm(x, FFN_RES_WEIGHT * _swiglu(x, *ffn1[layer]), *ln1[layer])
        if layer % 2 == 0:
            mix = _even_mixer(x, l0_mix_w_in, l0_conv_w, l0_mix_w_out)
        else:
            mix = _odd_mixer(x, l1_mix_w_in, l1_mlstm_b_i, l1_mlstm_b_f, l1_mlstm_norm_g, l1_mix_w_out)
        x = _post_norm(x, mix, *ln2[layer])
        x = _post_norm(x, FFN_RES_WEIGHT * _swiglu(x, *ffn2[layer]), *ln3[layer])
    return x
```

```python
import functools
import math

import jax
import jax.numpy as jnp
from jax import lax
from jax.experimental import pallas as pl
from jax.experimental.pallas import tpu as pltpu

F32 = jnp.float32
BF16 = jnp.bfloat16

DEPTH = 2
ALPHA = (2 * DEPTH) ** 0.25
FFN_RES_WEIGHT = 0.5
LN_EPS = 1e-5

LANES = 128
BLK = 128
CONV_CH = 256
CONV_WIDTH = 3
DSWA_HEAD_DIM = 64
DSWA_W = 768
DILATIONS = (1, 4, 16)
DSWA_SPAN = 128
MLSTM_HEADS = 4
MLSTM_W = 512
MLSTM_HEAD_DIM = 128
MLSTM_CHUNK = 128
SB_HEAD_DIM = 64
SB_W = 512
GATE_COLS = 2 * MLSTM_HEADS

TOKEN_TILE = 512
SB_TILE = 256
SB_LOG_FLOOR = -88.0
NEG_BIG = -1e30
VMEM_LIMIT = 56 * 1024 * 1024


def _params(*sem):
    return pltpu.CompilerParams(dimension_semantics=sem, vmem_limit_bytes=VMEM_LIMIT)


def _layer_norm(z, g, b):
    mu = jnp.mean(z, axis=-1, keepdims=True)
    zc = z - mu
    var = jnp.mean(zc * zc, axis=-1, keepdims=True)
    return zc * lax.rsqrt(var + LN_EPS) * g + b


def _dot(a, b):
    return jnp.dot(a, b, preferred_element_type=F32)


def _dot_nt(a, b):
    return lax.dot_general(a, b, (((1,), (1,)), ((), ())), preferred_element_type=F32)


def _dot_tn(a, b):
    return lax.dot_general(a, b, (((0,), (0,)), ((), ())), preferred_element_type=F32)


def _split_dot(tri, x):
    hi = x.astype(BF16)
    lo = (x - hi.astype(F32)).astype(BF16)
    return _dot(tri, hi) + _dot(tri, lo)


def _split_dot_r(x, tri):
    hi = x.astype(BF16)
    lo = (x - hi.astype(F32)).astype(BF16)
    return _dot(hi, tri) + _dot(lo, tri)


def _neg_softplus(z):
    return -(jnp.maximum(z, 0.0) + jnp.log(1.0 + jnp.exp(-jnp.abs(z))))


def _ffn_ln_body(x_ref, win_ref, wout_ref, g_ref, b_ref, o_ref, *, d_ff, tf):
    x = x_ref[...]
    xb = x.astype(BF16)
    acc = None
    for c in range(d_ff // tf):
        gate = _dot(xb, win_ref[:, c * tf:(c + 1) * tf])
        up = _dot(xb, win_ref[:, d_ff + c * tf:d_ff + (c + 1) * tf])
        act = (gate * jax.nn.sigmoid(gate) * up).astype(BF16)
        part = _dot(act, wout_ref[c * tf:(c + 1) * tf, :])
        acc = part if acc is None else acc + part
    z = ALPHA * x + FFN_RES_WEIGHT * acc
    o_ref[...] = _layer_norm(z, g_ref[...], b_ref[...])


def _ffn_ln(x, w_in, w_out, g, b):
    n, d = x.shape
    d_ff = w_out.shape[0]
    tf = d_ff // 2
    tm = TOKEN_TILE
    const = dict(pipeline_mode=pl.Buffered(1))
    return pl.pallas_call(
        functools.partial(_ffn_ln_body, d_ff=d_ff, tf=tf),
        grid=(n // tm,),
        in_specs=[
            pl.BlockSpec((tm, d), lambda i: (i, 0)),
            pl.BlockSpec((d, 2 * d_ff), lambda i: (0, 0), **const),
            pl.BlockSpec((d_ff, d), lambda i: (0, 0), **const),
            pl.BlockSpec((1, d), lambda i: (0, 0)),
            pl.BlockSpec((1, d), lambda i: (0, 0)),
        ],
        out_specs=pl.BlockSpec((tm, d), lambda i: (i, 0)),
        out_shape=jax.ShapeDtypeStruct((n, d), F32),
        compiler_params=_params("parallel"),
        name="ffn_ln",
    )(x, w_in, w_out, g.reshape(1, d), b.reshape(1, d))


def _proj_body(x_ref, w_ref, o_ref, *, tn):
    xb = x_ref[...].astype(BF16)
    for c in range(w_ref.shape[1] // tn):
        o_ref[:, c * tn:(c + 1) * tn] = _dot(xb, w_ref[:, c * tn:(c + 1) * tn]).astype(BF16)


def _proj(x, w):
    n, d = x.shape
    p = w.shape[1]
    tm = TOKEN_TILE
    return pl.pallas_call(
        functools.partial(_proj_body, tn=512),
        grid=(n // tm,),
        in_specs=[
            pl.BlockSpec((tm, d), lambda i: (i, 0)),
            pl.BlockSpec((d, p), lambda i: (0, 0), pipeline_mode=pl.Buffered(1)),
        ],
        out_specs=pl.BlockSpec((tm, p), lambda i: (i, 0)),
        out_shape=jax.ShapeDtypeStruct((n, p), BF16),
        compiler_params=_params("parallel"),
        name="proj_even",
    )(x, w)


def _proj_gates_body(x_ref, w_ref, wg_ref, wgt_ref, o_ref, gc_ref, gr_ref, *, tn):
    xb = x_ref[...].astype(BF16)
    for c in range(w_ref.shape[1] // tn):
        o_ref[:, c * tn:(c + 1) * tn] = _dot(xb, w_ref[:, c * tn:(c + 1) * tn]).astype(BF16)
    gc_ref[...] = _dot(xb, wg_ref[...])
    gr_ref[...] = _dot_nt(wgt_ref[...], xb)


def _proj_gates(x, w, wg, wgt):
    n, d = x.shape
    p = w.shape[1]
    tm = TOKEN_TILE
    const = dict(pipeline_mode=pl.Buffered(1))
    return pl.pallas_call(
        functools.partial(_proj_gates_body, tn=512),
        grid=(n // tm,),
        in_specs=[
            pl.BlockSpec((tm, d), lambda i: (i, 0)),
            pl.BlockSpec((d, p), lambda i: (0, 0), **const),
            pl.BlockSpec((d, LANES), lambda i: (0, 0), **const),
            pl.BlockSpec((GATE_COLS, d), lambda i: (0, 0), **const),
        ],
        out_specs=[
            pl.BlockSpec((tm, p), lambda i: (i, 0)),
            pl.BlockSpec((tm, LANES), lambda i: (i, 0)),
            pl.BlockSpec((GATE_COLS, tm), lambda i: (0, i)),
        ],
        out_shape=[
            jax.ShapeDtypeStruct((n, p), BF16),
            jax.ShapeDtypeStruct((n, LANES), F32),
            jax.ShapeDtypeStruct((GATE_COLS, n), F32),
        ],
        compiler_params=_params("parallel"),
        name="proj_odd",
    )(x, w, wg, wgt)


def _dilated_body(*refs, first, last, heads):
    q_ref, kp_ref, kc_ref, vp_ref, vc_ref = refs[:5]
    pos = 5
    if not first:
        acc_in, ml_in = refs[pos:pos + 2]
        pos += 2
    if last:
        (y_ref,) = refs[pos:]
    else:
        acc_out, ml_out = refs[pos:]

    n = pl.program_id(2)
    qi = lax.broadcasted_iota(jnp.int32, (BLK, 2 * BLK), 0)
    kj = lax.broadcasted_iota(jnp.int32, (BLK, 2 * BLK), 1)
    rel = qi + BLK - kj
    valid = (rel >= 0) & (rel <= DSWA_SPAN) & ((kj >= BLK) | (n > 0))
    lane = lax.broadcasted_iota(jnp.int32, (BLK, LANES), 1)
    low = lane < DSWA_HEAD_DIM
    scale = 1.0 / math.sqrt(DSWA_HEAD_DIM)

    ml_new = jnp.zeros((BLK, LANES), F32)
    ml_prev = None if first else ml_in[0]
    for hp in range(heads // 2):
        cols = slice(hp * LANES, (hp + 1) * LANES)
        q2 = q_ref[0, :, cols] * jnp.asarray(scale, BF16)
        k2 = jnp.concatenate([kp_ref[0, :, cols], kc_ref[0, :, cols]], axis=0)
        v2 = jnp.concatenate([vp_ref[0, :, cols], vc_ref[0, :, cols]], axis=0)
        acc_prev = None if first else acc_in[0, :, cols]
        halves = []
        for sub in range(2):
            h = 2 * hp + sub
            mine = low if sub == 0 else jnp.logical_not(low)
            qm = jnp.where(mine, q2, jnp.zeros_like(q2))
            s = jnp.where(valid, _dot_nt(qm, k2), NEG_BIG)
            m = jnp.max(s, axis=-1, keepdims=True)
            p = jnp.exp(s - m)
            l = jnp.sum(p, axis=-1, keepdims=True)
            pv = _dot(p.astype(BF16), v2)
            if not first:
                m_old = ml_prev[:, h:h + 1]
                l_old = ml_prev[:, heads + h:heads + h + 1]
                m_tot = jnp.maximum(m_old, m)
                a_old = jnp.exp(m_old - m_tot)
                a_new = jnp.exp(m - m_tot)
                l = l_old * a_old + l * a_new
                pv = acc_prev * a_old + pv * a_new
                m = m_tot
            if last:
                pv = pv / l
            else:
                ml_new = jnp.where(lane == h, m, ml_new)
                ml_new = jnp.where(lane == heads + h, l, ml_new)
            halves.append(pv)
        pair = jnp.where(low, halves[0], halves[1])
        if last:
            y_ref[0, :, cols] = pair.astype(BF16)
        else:
            acc_out[0, :, cols] = pair
    if not last:
        ml_out[0] = ml_new


def _dilated_attention(p, b, s):
    w = DSWA_W
    heads = w // DSWA_HEAD_DIM
    pcols = p.shape[1]
    cb = pcols // w
    acc = ml = None
    for idx, d in enumerate(DILATIONS):
        first, last = idx == 0, idx == len(DILATIONS) - 1
        nsub = s // d
        nb = nsub // BLK
        pv = p.reshape(b, nsub, d * pcols)
        cur = lambda col: (lambda bi, r, n: (bi, n, r * cb + col))
        prev = lambda col: (lambda bi, r, n: (bi, jnp.maximum(n - 1, 0), r * cb + col))
        blk = lambda width: (1, BLK, width)
        in_specs = [
            pl.BlockSpec(blk(w), cur(1)),
            pl.BlockSpec(blk(w), prev(2)),
            pl.BlockSpec(blk(w), cur(2)),
            pl.BlockSpec(blk(w), prev(3)),
            pl.BlockSpec(blk(w), cur(3)),
        ]
        args = [pv] * 5
        state_map = lambda bi, r, n: (bi, n, r)
        if not first:
            in_specs += [pl.BlockSpec(blk(w), state_map), pl.BlockSpec(blk(LANES), state_map)]
            args += [acc.reshape(b, nsub, d * w), ml.reshape(b, nsub, d * LANES)]
        if last:
            out_specs = pl.BlockSpec(blk(w), state_map)
            out_shape = jax.ShapeDtypeStruct((b, nsub, d * w), BF16)
        else:
            out_specs = [pl.BlockSpec(blk(w), state_map), pl.BlockSpec(blk(LANES), state_map)]
            out_shape = [jax.ShapeDtypeStruct((b, nsub, d * w), F32),
                         jax.ShapeDtypeStruct((b, nsub, d * LANES), F32)]
        out = pl.pallas_call(
            functools.partial(_dilated_body, first=first, last=last, heads=heads),
            grid=(b, d, nb),
            in_specs=in_specs,
            out_specs=out_specs,
            out_shape=out_shape,
            compiler_params=_params("parallel", "parallel", "arbitrary"),
            name=f"dilated_d{d}",
        )(*args)
        if last:
            return out.reshape(b * s, w)
        acc, ml = out


def _outproj_even_body(p_ref, halo_ref, y_ref, cw_ref, wo_ref, x_ref, g_ref, b_ref, o_ref, *, seq, tm, halo):
    c = CONV_CH
    pc = p_ref[...].astype(F32)
    bg, u = pc[:, :c], pc[:, c:2 * c] * pc[:, 2 * c:3 * c]
    ph = halo_ref[...].astype(F32)
    uh = ph[:, c:2 * c] * ph[:, 2 * c:3 * c]
    starts_sequence = (pl.program_id(0) * tm) % seq == 0
    uh = jnp.where(starts_sequence, 0.0, uh)
    full = jnp.concatenate([uh, u], axis=0)
    cw = cw_ref[...]
    conv = cw[2:3, :] * u + cw[1:2, :] * full[halo - 1:halo - 1 + tm] + cw[0:1, :] * full[halo - 2:halo - 2 + tm]
    y_conv = (bg * conv).astype(BF16)
    mix = _dot(y_conv, wo_ref[:c, :]) + _dot(y_ref[...], wo_ref[c:, :])
    z = ALPHA * x_ref[...] + mix
    o_ref[...] = _layer_norm(z, g_ref[...], b_ref[...])


def _outproj_even(p, y_att, conv_w, w_out, x, g, b, seq):
    n, d = x.shape
    tm = TOKEN_TILE
    halo = 16
    cw = 3 * CONV_CH
    return pl.pallas_call(
        functools.partial(_outproj_even_body, seq=seq, tm=tm, halo=halo),
        grid=(n // tm,),
        in_specs=[
            pl.BlockSpec((tm, cw), lambda i: (i, 0)),
            pl.BlockSpec((halo, cw), lambda i: (jnp.maximum(i * (tm // halo) - 1, 0), 0)),
            pl.BlockSpec((tm, DSWA_W), lambda i: (i, 0)),
            pl.BlockSpec((CONV_WIDTH, CONV_CH), lambda i: (0, 0)),
            pl.BlockSpec((d, d), lambda i: (0, 0), pipeline_mode=pl.Buffered(1)),
            pl.BlockSpec((tm, d), lambda i: (i, 0)),
            pl.BlockSpec((1, d), lambda i: (0, 0)),
            pl.BlockSpec((1, d), lambda i: (0, 0)),
        ],
        out_specs=pl.BlockSpec((tm, d), lambda i: (i, 0)),
        out_shape=jax.ShapeDtypeStruct((n, d), F32),
        compiler_params=_params("parallel"),
        name="outproj_even",
    )(p, p, y_att, conv_w, w_out, x, g.reshape(1, d), b.reshape(1, d))


def _mlstm_body(p_ref, gc_ref, gr_ref, bc_ref, br_ref, ng_ref, y_ref, c_scr, n_scr, m_scr):
    L, dh, nh = MLSTM_CHUNK, MLSTM_HEAD_DIM, MLSTM_HEADS

    @pl.when(pl.program_id(1) == 0)
    def _():
        c_scr[...] = jnp.zeros_like(c_scr)
        n_scr[...] = jnp.zeros_like(n_scr)
        m_scr[...] = jnp.zeros_like(m_scr)

    row = lax.broadcasted_iota(jnp.int32, (L, L), 0)
    col = lax.broadcasted_iota(jnp.int32, (L, L), 1)
    causal = col <= row
    tri = causal.astype(BF16)
    tri_t = (row <= col).astype(BF16)

    gcol = gc_ref[...] + bc_ref[...]
    grow = gr_ref[...] + br_ref[...]
    bcum_c = _split_dot(tri, _neg_softplus(-gcol))
    bcum_r = _split_dot_r(_neg_softplus(-grow), tri_t)
    scale = 1.0 / math.sqrt(dh)

    for h in range(nh):
        cols = slice(h * dh, (h + 1) * dh)
        q = p_ref[:, cols]
        k = p_ref[:, MLSTM_W + h * dh:MLSTM_W + (h + 1) * dh]
        v = p_ref[:, 2 * MLSTM_W + h * dh:2 * MLSTM_W + (h + 1) * dh]
        o = p_ref[:, 3 * MLSTM_W + h * dh:3 * MLSTM_W + (h + 1) * dh]
        i_c, i_r = gcol[:, h:h + 1], grow[h:h + 1, :]
        b_c, b_r = bcum_c[:, nh + h:nh + h + 1], bcum_r[nh + h:nh + h + 1, :]
        m_prev = m_scr[h][:, 0:1]
        c_prev = c_scr[h]
        n_prev = n_scr[h]

        d_intra = jnp.where(causal, b_c - b_r + i_r, NEG_BIG)
        d_inter = b_c + m_prev
        m_t = jnp.maximum(d_inter, jnp.max(d_intra, axis=-1, keepdims=True))
        w_inter = jnp.exp(d_inter - m_t)
        qk = _dot_nt(q, k) * scale * jnp.exp(d_intra - m_t)
        num = w_inter * _dot(q, c_prev.astype(BF16)) + _dot(qk.astype(BF16), v)
        den = (w_inter * jnp.sum(q.astype(F32) * n_prev, axis=-1, keepdims=True)
               + jnp.sum(qk, axis=-1, keepdims=True))
        h_tilde = num / jnp.maximum(jnp.abs(den), jnp.exp(-m_t))

        b_tot = b_r[:, L - 1:L]
        d_state = b_tot - b_c + i_c
        m_new = jnp.maximum(b_tot + m_prev, jnp.max(d_state, axis=0, keepdims=True))
        w_s = jnp.exp(d_state - m_new) * scale
        decay = jnp.exp(b_tot + m_prev - m_new)
        kw = k.astype(F32) * w_s
        c_scr[h] = decay * c_prev + _dot_tn(kw.astype(BF16), v)
        n_scr[h] = decay * n_prev + jnp.sum(kw, axis=0, keepdims=True)
        m_scr[h] = jnp.broadcast_to(m_new, (1, LANES))

        cell = jax.nn.sigmoid(o.astype(F32)) * h_tilde
        mu = jnp.mean(cell, axis=-1, keepdims=True)
        cc = cell - mu
        var = jnp.mean(cc * cc, axis=-1, keepdims=True)
        y_ref[:, cols] = (cc * lax.rsqrt(var + LN_EPS) * ng_ref[:, cols]).astype(BF16)


def _mlstm(p, gcol, grow, b_i, b_f, norm_g, b, s):
    L = MLSTM_CHUNK
    nc = s // L
    bias = jnp.concatenate([b_i, b_f]).astype(F32)
    bias_c = jnp.zeros((1, LANES), F32).at[0, :GATE_COLS].set(bias)
    bias_r = bias.reshape(GATE_COLS, 1)
    tok = lambda bi, c: (bi * nc + c, 0)
    return pl.pallas_call(
        _mlstm_body,
        grid=(b, nc),
        in_specs=[
            pl.BlockSpec((L, 4 * MLSTM_W), tok),
            pl.BlockSpec((L, LANES), tok),
            pl.BlockSpec((GATE_COLS, L), lambda bi, c: (0, bi * nc + c)),
            pl.BlockSpec((1, LANES), lambda bi, c: (0, 0)),
            pl.BlockSpec((GATE_COLS, 1), lambda bi, c: (0, 0)),
            pl.BlockSpec((1, MLSTM_W), lambda bi, c: (0, 0)),
        ],
        out_specs=pl.BlockSpec((L, MLSTM_W), tok),
        out_shape=jax.ShapeDtypeStruct((b * s, MLSTM_W), BF16),
        scratch_shapes=[
            pltpu.VMEM((MLSTM_HEADS, MLSTM_HEAD_DIM, MLSTM_HEAD_DIM), F32),
            pltpu.VMEM((MLSTM_HEADS, 1, MLSTM_HEAD_DIM), F32),
            pltpu.VMEM((MLSTM_HEADS, 1, LANES), F32),
        ],
        compiler_params=_params("parallel", "arbitrary"),
        name="mlstm",
    )(p, gcol, grow, bias_c, bias_r, norm_g.reshape(1, MLSTM_W))


def _stick_body(q_ref, k_ref, v_ref, y_ref, acc_scr, c_scr):
    t = SB_TILE
    qb = pl.program_id(2)
    row = lax.broadcasted_iota(jnp.int32, (t, t), 0)
    col = lax.broadcasted_iota(jnp.int32, (t, t), 1)
    suffix = (row >= col).astype(BF16)
    lane = lax.broadcasted_iota(jnp.int32, (t, LANES), 1)
    low = lane < SB_HEAD_DIM
    q2 = q_ref[0] * jnp.asarray(1.0 / math.sqrt(SB_HEAD_DIM), BF16)

    for sub in range(2):
        mine = low if sub == 0 else jnp.logical_not(low)
        qm = jnp.where(mine, q2, jnp.zeros_like(q2))
        acc_scr[...] = jnp.zeros_like(acc_scr)
        c_scr[...] = jnp.zeros_like(c_scr)

        def cond(carry):
            j, c_max = carry
            return (j >= 0) & (c_max > SB_LOG_FLOOR)

        def body(carry):
            j, _ = carry
            ks = pl.multiple_of(j * t, t)
            kb = k_ref[0, pl.ds(ks, t), :]
            vb = v_ref[0, pl.ds(ks, t), :]
            z = _dot_nt(qm, kb)
            before = (col < row) | (j < qb)
            lg = jnp.where(before, _neg_softplus(z), 0.0)
            incl = _split_dot_r(lg, suffix)
            c_old = c_scr[...]
            a = jnp.where(before, jnp.exp(z + incl + c_old), 0.0)
            acc_scr[...] += _dot(a.astype(BF16), vb)
            c_new = c_old + incl[:, 0:1]
            c_scr[...] = c_new
            return j - 1, jnp.max(c_new)

        lax.while_loop(cond, body, (qb, jnp.float32(0.0)))
        if sub == 0:
            first = acc_scr[...]
    y_ref[0] = jnp.where(low, first, acc_scr[...]).astype(BF16)


def _stick_breaking(p, b, s, col0):
    t = SB_TILE
    pairs = SB_W // LANES
    pv = p.reshape(b, s, p.shape[1])
    c0 = col0 // LANES
    return pl.pallas_call(
        _stick_body,
        grid=(b, pairs, s // t),
        in_specs=[
            pl.BlockSpec((1, t, LANES), lambda bi, hp, qi: (bi, qi, c0 + hp)),
            pl.BlockSpec((1, s, LANES), lambda bi, hp, qi: (bi, 0, c0 + pairs + hp)),
            pl.BlockSpec((1, s, LANES), lambda bi, hp, qi: (bi, 0, c0 + 2 * pairs + hp)),
        ],
        out_specs=pl.BlockSpec((1, t, LANES), lambda bi, hp, qi: (bi, qi, hp)),
        out_shape=jax.ShapeDtypeStruct((b, s, SB_W), BF16),
        scratch_shapes=[pltpu.VMEM((t, LANES), F32), pltpu.VMEM((t, 1), F32)],
        compiler_params=_params("parallel", "parallel", "arbitrary"),
        name="stick_breaking",
    )(pv, pv, pv).reshape(b * s, SB_W)


def _outproj_odd_body(ym_ref, ys_ref, wo_ref, x_ref, g_ref, b_ref, o_ref):
    mix = _dot(ym_ref[...], wo_ref[:MLSTM_W, :]) + _dot(ys_ref[...], wo_ref[MLSTM_W:, :])
    z = ALPHA * x_ref[...] + mix
    o_ref[...] = _layer_norm(z, g_ref[...], b_ref[...])


def _outproj_odd(y_m, y_s, w_out, x, g, b):
    n, d = x.shape
    tm = TOKEN_TILE
    return pl.pallas_call(
        _outproj_odd_body,
        grid=(n // tm,),
        in_specs=[
            pl.BlockSpec((tm, MLSTM_W), lambda i: (i, 0)),
            pl.BlockSpec((tm, SB_W), lambda i: (i, 0)),
            pl.BlockSpec((d, d), lambda i: (0, 0), pipeline_mode=pl.Buffered(1)),
            pl.BlockSpec((tm, d), lambda i: (i, 0)),
            pl.BlockSpec((1, d), lambda i: (0, 0)),
            pl.BlockSpec((1, d), lambda i: (0, 0)),
        ],
        out_specs=pl.BlockSpec((tm, d), lambda i: (i, 0)),
        out_shape=jax.ShapeDtypeStruct((n, d), F32),
        compiler_params=_params("parallel"),
        name="outproj_odd",
    )(y_m, y_s, w_out, x, g.reshape(1, d), b.reshape(1, d))


def _even_mixer(x, w_in, conv_w, w_out, g, b, batch, seq):
    p = _proj(x, w_in.astype(BF16))
    y_att = _dilated_attention(p, batch, seq)
    return _outproj_even(p, y_att, conv_w, w_out.astype(BF16), x, g, b, seq)


def _odd_mixer(x, w_in, b_i, b_f, norm_g, w_out, g, b, batch, seq):
    gate0 = 4 * MLSTM_W
    w_main = jnp.concatenate([w_in[:, :gate0], w_in[:, gate0 + GATE_COLS:]], axis=1).astype(BF16)
    w_gate = w_in[:, gate0:gate0 + GATE_COLS].astype(BF16)
    w_gate_pad = jnp.pad(w_gate, ((0, 0), (0, LANES - GATE_COLS)))
    p, gcol, grow = _proj_gates(x, w_main, w_gate_pad, w_gate.T)
    y_m = _mlstm(p, gcol, grow, b_i, b_f, norm_g, batch, seq)
    y_s = _stick_breaking(p, batch, seq, gate0)
    return _outproj_odd(y_m, y_s, w_out.astype(BF16), x, g, b)


def kernel(x, l0_ffn1_w_in, l0_ffn1_w_out, l0_ln1_g, l0_ln1_b, l0_mix_w_in, l0_conv_w, l0_mix_w_out, l0_ln2_g, l0_ln2_b, l0_ffn2_w_in, l0_ffn2_w_out, l0_ln3_g, l0_ln3_b, l1_ffn1_w_in, l1_ffn1_w_out, l1_ln1_g, l1_ln1_b, l1_mix_w_in, l1_mlstm_b_i, l1_mlstm_b_f, l1_mlstm_norm_g, l1_mix_w_out, l1_ln2_g, l1_ln2_b, l1_ffn2_w_in, l1_ffn2_w_out, l1_ln3_g, l1_ln3_b):
    batch, seq, d = x.shape
    ffn = lambda t, w_in, w_out, g, b: _ffn_ln(t, w_in.astype(BF16), w_out.astype(BF16), g, b)
    t = x.reshape(batch * seq, d)
    t = ffn(t, l0_ffn1_w_in, l0_ffn1_w_out, l0_ln1_g, l0_ln1_b)
    t = _even_mixer(t, l0_mix_w_in, l0_conv_w, l0_mix_w_out, l0_ln2_g, l0_ln2_b, batch, seq)
    t = ffn(t, l0_ffn2_w_in, l0_ffn2_w_out, l0_ln3_g, l0_ln3_b)
    t = ffn(t, l1_ffn1_w_in, l1_ffn1_w_out, l1_ln1_g, l1_ln1_b)
    t = _odd_mixer(t, l1_mix_w_in, l1_mlstm_b_i, l1_mlstm_b_f, l1_mlstm_norm_g, l1_mix_w_out,
                   l1_ln2_g, l1_ln2_b, batch, seq)
    t = ffn(t, l1_ffn2_w_in, l1_ffn2_w_out, l1_ln3_g, l1_ln3_b)
    return t.reshape(batch, seq, d)
```

```python
import functools
import math

import jax
import jax.numpy as jnp
import numpy as np
from jax import lax
from jax.experimental import pallas as pl
from jax.experimental.pallas import tpu as pltpu

F32 = jnp.float32
BF16 = jnp.bfloat16

DEPTH = 2
ALPHA = (2 * DEPTH) ** 0.25
FFN_RES_WEIGHT = 0.5
LN_EPS = 1e-5

LANES = 128
BLK = 128
CONV_CH = 256
CONV_WIDTH = 3
DSWA_HEAD_DIM = 64
DSWA_W = 768
DSWA_SPAN = 128
DSWA_PERIOD = 16
DSWA_SUPER = DSWA_PERIOD * BLK
MLSTM_HEADS = 4
MLSTM_W = 512
MLSTM_HEAD_DIM = 128
MLSTM_CHUNK = 128
SB_HEAD_DIM = 64
SB_W = 512
GATE_COLS = 2 * MLSTM_HEADS

TOKEN_TILE = 512
SB_TILE = 256
SB_LOG_FLOOR = -88.0
NEG_BIG = -1e30
VMEM_LIMIT = 56 * 1024 * 1024


def _params(*sem):
    return pltpu.CompilerParams(dimension_semantics=sem, vmem_limit_bytes=VMEM_LIMIT)


def _layer_norm(z, g, b):
    mu = jnp.mean(z, axis=-1, keepdims=True)
    zc = z - mu
    var = jnp.mean(zc * zc, axis=-1, keepdims=True)
    return zc * lax.rsqrt(var + LN_EPS) * g + b


def _dot(a, b):
    return jnp.dot(a, b, preferred_element_type=F32)


def _dot_nt(a, b):
    return lax.dot_general(a, b, (((1,), (1,)), ((), ())), preferred_element_type=F32)


def _dot_tn(a, b):
    return lax.dot_general(a, b, (((0,), (0,)), ((), ())), preferred_element_type=F32)


def _split_dot(tri, x):
    hi = x.astype(BF16)
    lo = (x - hi.astype(F32)).astype(BF16)
    return _dot(tri, hi) + _dot(tri, lo)


def _split_dot_r(x, tri):
    hi = x.astype(BF16)
    lo = (x - hi.astype(F32)).astype(BF16)
    return _dot(hi, tri) + _dot(lo, tri)


def _neg_softplus(z):
    return -(jnp.maximum(z, 0.0) + jnp.log(1.0 + jnp.exp(-jnp.abs(z))))


def _ffn_ln_body(x_ref, win_ref, wout_ref, g_ref, b_ref, o_ref, *, d_ff, tf):
    x = x_ref[...]
    xb = x.astype(BF16)
    acc = None
    for c in range(d_ff // tf):
        gate = _dot(xb, win_ref[:, c * tf:(c + 1) * tf])
        up = _dot(xb, win_ref[:, d_ff + c * tf:d_ff + (c + 1) * tf])
        act = (gate * jax.nn.sigmoid(gate) * up).astype(BF16)
        part = _dot(act, wout_ref[c * tf:(c + 1) * tf, :])
        acc = part if acc is None else acc + part
    z = ALPHA * x + FFN_RES_WEIGHT * acc
    o_ref[...] = _layer_norm(z, g_ref[...], b_ref[...])


def _ffn_ln(x, w_in, w_out, g, b):
    n, d = x.shape
    d_ff = w_out.shape[0]
    tf = d_ff // 2
    tm = TOKEN_TILE
    const = dict(pipeline_mode=pl.Buffered(1))
    return pl.pallas_call(
        functools.partial(_ffn_ln_body, d_ff=d_ff, tf=tf),
        grid=(n // tm,),
        in_specs=[
            pl.BlockSpec((tm, d), lambda i: (i, 0)),
            pl.BlockSpec((d, 2 * d_ff), lambda i: (0, 0), **const),
            pl.BlockSpec((d_ff, d), lambda i: (0, 0), **const),
            pl.BlockSpec((1, d), lambda i: (0, 0)),
            pl.BlockSpec((1, d), lambda i: (0, 0)),
        ],
        out_specs=pl.BlockSpec((tm, d), lambda i: (i, 0)),
        out_shape=jax.ShapeDtypeStruct((n, d), F32),
        compiler_params=_params("parallel"),
        name="ffn_ln",
    )(x, w_in, w_out, g.reshape(1, d), b.reshape(1, d))


def _proj_even_body(x_ref, w_ref, pc_ref, pq_ref, slab, *, tm, tn):
    cw = 3 * CONV_CH
    xb = x_ref[...].astype(BF16)
    pc_ref[...] = _dot(xb, w_ref[:, :cw]).astype(BF16)
    per = tn // LANES
    for c in range((w_ref.shape[1] - cw) // tn):
        res = _dot(xb, w_ref[:, cw + c * tn:cw + (c + 1) * tn])
        for j in range(per):
            slab[c * per + j] = res[:, j * LANES:(j + 1) * LANES]
    rows = tm // DSWA_PERIOD
    for r in range(DSWA_PERIOD):
        for c in range(slab.shape[0]):
            piece = slab[c, pl.ds(r, rows, stride=DSWA_PERIOD), :]
            pq_ref[0, r, :, c * LANES:(c + 1) * LANES] = piece.astype(BF16)


def _proj_even(x, w):
    n, d = x.shape
    cw = 3 * CONV_CH
    qw = w.shape[1] - cw
    tm = TOKEN_TILE
    tiles = DSWA_SUPER // tm
    rows = tm // DSWA_PERIOD
    return pl.pallas_call(
        functools.partial(_proj_even_body, tm=tm, tn=256),
        grid=(n // tm,),
        in_specs=[
            pl.BlockSpec((tm, d), lambda i: (i, 0)),
            pl.BlockSpec((d, cw + qw), lambda i: (0, 0), pipeline_mode=pl.Buffered(1)),
        ],
        out_specs=[
            pl.BlockSpec((tm, cw), lambda i: (i, 0)),
            pl.BlockSpec((1, DSWA_PERIOD, rows, qw), lambda i: (i // tiles, 0, i % tiles, 0)),
        ],
        out_shape=[
            jax.ShapeDtypeStruct((n, cw), BF16),
            jax.ShapeDtypeStruct((n // DSWA_SUPER, DSWA_PERIOD, DSWA_SUPER // DSWA_PERIOD, qw), BF16),
        ],
        scratch_shapes=[pltpu.VMEM((qw // LANES, tm, LANES), F32)],
        compiler_params=_params("parallel"),
        name="proj_even",
    )(x, w)


def _proj_gates_body(x_ref, w_ref, wg_ref, wgt_ref, o_ref, gc_ref, gr_ref, *, tn):
    xb = x_ref[...].astype(BF16)
    for c in range(w_ref.shape[1] // tn):
        o_ref[:, c * tn:(c + 1) * tn] = _dot(xb, w_ref[:, c * tn:(c + 1) * tn]).astype(BF16)
    gc_ref[...] = _dot(xb, wg_ref[...])
    gr_ref[...] = _dot_nt(wgt_ref[...], xb)


def _proj_gates(x, w, wg, wgt):
    n, d = x.shape
    p = w.shape[1]
    tm = TOKEN_TILE
    const = dict(pipeline_mode=pl.Buffered(1))
    return pl.pallas_call(
        functools.partial(_proj_gates_body, tn=512),
        grid=(n // tm,),
        in_specs=[
            pl.BlockSpec((tm, d), lambda i: (i, 0)),
            pl.BlockSpec((d, p), lambda i: (0, 0), **const),
            pl.BlockSpec((d, LANES), lambda i: (0, 0), **const),
            pl.BlockSpec((GATE_COLS, d), lambda i: (0, 0), **const),
        ],
        out_specs=[
            pl.BlockSpec((tm, p), lambda i: (i, 0)),
            pl.BlockSpec((tm, LANES), lambda i: (i, 0)),
            pl.BlockSpec((GATE_COLS, tm), lambda i: (0, i)),
        ],
        out_shape=[
            jax.ShapeDtypeStruct((n, p), BF16),
            jax.ShapeDtypeStruct((n, LANES), F32),
            jax.ShapeDtypeStruct((GATE_COLS, n), F32),
        ],
        compiler_params=_params("parallel"),
        name="proj_odd",
    )(x, w, wg, wgt)


def _dilated_biases():
    def make(q_pos, k_pos, k_is_prev):
        rel = q_pos[:, None] - k_pos[None, :]
        ok = (rel >= 0) & (rel <= DSWA_SPAN)
        variants = [ok, ok & ~k_is_prev[None, :]]
        return jnp.asarray(np.stack([np.where(v, 0.0, NEG_BIG) for v in variants]), F32)

    per = DSWA_PERIOD
    kj = np.arange(2 * BLK)
    b16 = make(np.arange(BLK), kj - BLK, kj < BLK)
    qa, qi = np.divmod(np.arange(BLK), BLK // 4)
    ka, ki = np.divmod(np.arange(2 * BLK), 2 * BLK // 4)
    b4 = make(4 * qi + qa, 4 * (ki - BLK // 4) + ka, ki < BLK // 4)
    qg, qi = np.divmod(np.arange(2 * BLK), 2 * BLK // per)
    kg, ki = np.divmod(np.arange(4 * BLK), 4 * BLK // per)
    b1 = make(per * qi + qg, per * (ki - 2 * BLK // per) + kg, ki < 2 * BLK // per)
    return b16, b4, b1


def _dilated_body(b16_ref, b4_ref, b1_ref, q_ref, kp_ref, kc_ref, vp_ref, vc_ref, y_ref,
                  kf, vf, acc, m_s, l_s, nat):
    per = DSWA_PERIOD
    first = (pl.program_id(1) == 0).astype(jnp.int32)
    kf[:, :BLK, :] = kp_ref[0]
    kf[:, BLK:, :] = kc_ref[0]
    vf[:, :BLK, :] = vp_ref[0]
    vf[:, BLK:, :] = vc_ref[0]
    scale = jnp.asarray(1.0 / math.sqrt(DSWA_HEAD_DIM), BF16)

    def low_mask(rows):
        return lax.broadcasted_iota(jnp.int32, (rows, LANES), 1) < DSWA_HEAD_DIM

    def attend(units):
        scores = []
        for q2, kk, _, _ in units:
            low = low_mask(q2.shape[0])
            zero = jnp.zeros_like(q2)
            stacked = jnp.concatenate([jnp.where(low, q2, zero), jnp.where(low, zero, q2)], axis=0)
            scores.append(_dot_nt(stacked, kk))
        soft = []
        for s, (_, _, _, bias) in zip(scores, units):
            s = s + jnp.concatenate([bias, bias], axis=0)
            m = jnp.max(s, axis=-1, keepdims=True)
            p = jnp.exp(s - m)
            soft.append((m, jnp.sum(p, axis=-1, keepdims=True), p.astype(BF16)))
        return [(m, l, _dot(p, vv)) for (m, l, p), (_, _, vv, _) in zip(soft, units)]

    def merge(part, m_old, l_old, acc_old):
        m, l, pv = part
        r = acc_old.shape[0]
        low = low_mask(r)
        m_tot = jnp.maximum(m_old, m)
        a_old = jnp.exp(m_old - m_tot)
        a_new = jnp.exp(m - m_tot)
        fresh = pv * a_new
        acc_new = acc_old * jnp.where(low, a_old[:r], a_old[r:]) + jnp.where(low, fresh[:r], fresh[r:])
        return m_tot, l_old * a_old + l * a_new, acc_new

    def gather(ref, lead, groups, rows):
        return jnp.concatenate([ref[lead + (g, rows, slice(None))] for g in groups], axis=0)

    def gather_heads(ref, groups, rows):
        return jnp.concatenate([gather(ref, (sub,), groups, rows) for sub in range(2)], axis=0)

    def scatter(ref, lead, groups, rows, value):
        n = value.shape[0] // len(groups)
        for j, g in enumerate(groups):
            ref[lead + (g, rows, slice(None))] = value[j * n:(j + 1) * n]

    def scatter_heads(ref, groups, rows, value):
        r = value.shape[0] // 2
        for sub in range(2):
            scatter(ref, (sub,), groups, rows, value[sub * r:(sub + 1) * r])

    def fold(parts, where):
        merged = []
        for part, (groups, rows) in zip(parts, where):
            m, l, a = merge(part, gather_heads(m_s, groups, rows), gather_heads(l_s, groups, rows),
                            gather(acc, (), groups, rows))
            merged.append((m, l, a, groups, rows))
        return merged

    full = slice(None)

    def d16(it, carry):
        groups = [4 * it + j for j in range(4)]
        parts = attend([(q_ref[0, g] * scale, kf[g], vf[g], b16_ref[first]) for g in groups])
        for g, (m, l, pv) in zip(groups, parts):
            scatter_heads(m_s, [g], full, jnp.broadcast_to(m, (2 * BLK, LANES)))
            scatter_heads(l_s, [g], full, jnp.broadcast_to(l, (2 * BLK, LANES)))
            acc[g] = jnp.where(low_mask(BLK), pv[:BLK], pv[BLK:])
        return carry

    lax.fori_loop(0, per // 4, d16, 0)

    def d4(r4, carry):
        nq = BLK // 4
        groups = [4 * a + r4 for a in range(4)]
        units, where = [], []
        for blk in range(4):
            qrows = slice(blk * nq, (blk + 1) * nq)
            krows = slice(BLK - nq + blk * nq, BLK + (blk + 1) * nq)
            bias = b4_ref[first] if blk == 0 else b4_ref[0]
            units.append((gather(q_ref, (0,), groups, qrows) * scale,
                          gather(kf, (), groups, krows), gather(vf, (), groups, krows), bias))
            where.append((groups, qrows))
        for m, l, a, groups, rows in fold(attend(units), where):
            scatter_heads(m_s, groups, rows, m)
            scatter_heads(l_s, groups, rows, l)
            scatter(acc, (), groups, rows, a)
        return carry

    lax.fori_loop(0, 4, d4, 0)

    def d1(it, carry):
        nq = 2 * BLK // per
        groups = list(range(per))
        units, where, blks = [], [], []
        for j in range(2):
            blk = 2 * it + j
            qrows = pl.ds(pl.multiple_of(blk * nq, nq), nq)
            krows = pl.ds(pl.multiple_of(BLK - nq + blk * nq, nq), 2 * nq)
            flag = first * (blk == 0).astype(jnp.int32)
            units.append((gather(q_ref, (0,), groups, qrows) * scale,
                          gather(kf, (), groups, krows), gather(vf, (), groups, krows), b1_ref[flag]))
            where.append((groups, qrows))
            blks.append(blk)
        for blk, (m, l, a, _, _) in zip(blks, fold(attend(units), where)):
            r = a.shape[0]
            out = a / jnp.where(low_mask(r), l[:r], l[r:])
            for g in groups:
                nat[pl.ds(g, nq, stride=per), :] = out[g * nq:(g + 1) * nq]
            y_ref[0, pl.ds(pl.multiple_of(blk * r, r), r), :] = nat[...].astype(BF16)
        return carry

    lax.fori_loop(0, DSWA_SUPER // (4 * BLK), d1, 0)


def _dilated_attention(pq, b, s):
    per, w = DSWA_PERIOD, DSWA_W
    pairs = w // LANES
    nsb = s // DSWA_SUPER
    rows = DSWA_SUPER // per
    biases = _dilated_biases()
    blk = (1, per, rows, LANES)
    cur = lambda col: (lambda bi, sb, hp: (bi * nsb + sb, 0, 0, col * pairs + hp))
    prev = lambda col: (lambda bi, sb, hp: (bi * nsb + jnp.maximum(sb - 1, 0), 0, 0, col * pairs + hp))
    const = lambda a: pl.BlockSpec(a.shape, lambda bi, sb, hp: (0, 0, 0))
    return pl.pallas_call(
        _dilated_body,
        grid=(b, nsb, pairs),
        in_specs=[const(a) for a in biases] + [
            pl.BlockSpec(blk, cur(0)),
            pl.BlockSpec(blk, prev(1)),
            pl.BlockSpec(blk, cur(1)),
            pl.BlockSpec(blk, prev(2)),
            pl.BlockSpec(blk, cur(2)),
        ],
        out_specs=pl.BlockSpec((1, DSWA_SUPER, LANES), lambda bi, sb, hp: (bi * nsb + sb, 0, hp)),
        out_shape=jax.ShapeDtypeStruct((b * nsb, DSWA_SUPER, w), BF16),
        scratch_shapes=[
            pltpu.VMEM((per, 2 * rows, LANES), BF16),
            pltpu.VMEM((per, 2 * rows, LANES), BF16),
            pltpu.VMEM((per, rows, LANES), F32),
            pltpu.VMEM((2, per, rows, LANES), F32),
            pltpu.VMEM((2, per, rows, LANES), F32),
            pltpu.VMEM((2 * BLK, LANES), F32),
        ],
        compiler_params=_params("parallel", "parallel", "parallel"),
        name="dilated",
    )(*biases, pq, pq, pq, pq, pq).reshape(b * s, w)


def _outproj_even_body(p_ref, halo_ref, y_ref, cw_ref, wo_ref, x_ref, g_ref, b_ref, o_ref, *, seq, tm, halo):
    c = CONV_CH
    pc = p_ref[...].astype(F32)
    bg, u = pc[:, :c], pc[:, c:2 * c] * pc[:, 2 * c:3 * c]
    ph = halo_ref[...].astype(F32)
    uh = ph[:, c:2 * c] * ph[:, 2 * c:3 * c]
    starts_sequence = (pl.program_id(0) * tm) % seq == 0
    uh = jnp.where(starts_sequence, 0.0, uh)
    full = jnp.concatenate([uh, u], axis=0)
    cw = cw_ref[...]
    conv = cw[2:3, :] * u + cw[1:2, :] * full[halo - 1:halo - 1 + tm] + cw[0:1, :] * full[halo - 2:halo - 2 + tm]
    y_conv = (bg * conv).astype(BF16)
    mix = _dot(y_conv, wo_ref[:c, :]) + _dot(y_ref[...], wo_ref[c:, :])
    z = ALPHA * x_ref[...] + mix
    o_ref[...] = _layer_norm(z, g_ref[...], b_ref[...])


def _outproj_even(p, y_att, conv_w, w_out, x, g, b, seq):
    n, d = x.shape
    tm = TOKEN_TILE
    halo = 16
    cw = 3 * CONV_CH
    return pl.pallas_call(
        functools.partial(_outproj_even_body, seq=seq, tm=tm, halo=halo),
        grid=(n // tm,),
        in_specs=[
            pl.BlockSpec((tm, cw), lambda i: (i, 0)),
            pl.BlockSpec((halo, cw), lambda i: (jnp.maximum(i * (tm // halo) - 1, 0), 0)),
            pl.BlockSpec((tm, DSWA_W), lambda i: (i, 0)),
            pl.BlockSpec((CONV_WIDTH, CONV_CH), lambda i: (0, 0)),
            pl.BlockSpec((d, d), lambda i: (0, 0), pipeline_mode=pl.Buffered(1)),
            pl.BlockSpec((tm, d), lambda i: (i, 0)),
            pl.BlockSpec((1, d), lambda i: (0, 0)),
            pl.BlockSpec((1, d), lambda i: (0, 0)),
        ],
        out_specs=pl.BlockSpec((tm, d), lambda i: (i, 0)),
        out_shape=jax.ShapeDtypeStruct((n, d), F32),
        compiler_params=_params("parallel"),
        name="outproj_even",
    )(p, p, y_att, conv_w, w_out, x, g.reshape(1, d), b.reshape(1, d))


def _mlstm_body(p_ref, gc_ref, gr_ref, bc_ref, br_ref, ng_ref, y_ref, c_scr, n_scr, m_scr):
    L, dh, nh = MLSTM_CHUNK, MLSTM_HEAD_DIM, MLSTM_HEADS

    @pl.when(pl.program_id(1) == 0)
    def _():
        c_scr[...] = jnp.zeros_like(c_scr)
        n_scr[...] = jnp.zeros_like(n_scr)
        m_scr[...] = jnp.zeros_like(m_scr)

    row = lax.broadcasted_iota(jnp.int32, (L, L), 0)
    col = lax.broadcasted_iota(jnp.int32, (L, L), 1)
    causal = col <= row
    tri = causal.astype(BF16)
    tri_t = (row <= col).astype(BF16)

    gcol = gc_ref[...] + bc_ref[...]
    grow = gr_ref[...] + br_ref[...]
    bcum_c = _split_dot(tri, _neg_softplus(-gcol))
    bcum_r = _split_dot_r(_neg_softplus(-grow), tri_t)
    scale = 1.0 / math.sqrt(dh)

    for h in range(nh):
        cols = slice(h * dh, (h + 1) * dh)
        q = p_ref[:, cols]
        k = p_ref[:, MLSTM_W + h * dh:MLSTM_W + (h + 1) * dh]
        v = p_ref[:, 2 * MLSTM_W + h * dh:2 * MLSTM_W + (h + 1) * dh]
        o = p_ref[:, 3 * MLSTM_W + h * dh:3 * MLSTM_W + (h + 1) * dh]
        i_c, i_r = gcol[:, h:h + 1], grow[h:h + 1, :]
        b_c, b_r = bcum_c[:, nh + h:nh + h + 1], bcum_r[nh + h:nh + h + 1, :]
        m_prev = m_scr[h][:, 0:1]
        c_prev = c_scr[h]
        n_prev = n_scr[h]

        d_intra = jnp.where(causal, b_c - b_r + i_r, NEG_BIG)
        d_inter = b_c + m_prev
        m_t = jnp.maximum(d_inter, jnp.max(d_intra, axis=-1, keepdims=True))
        w_inter = jnp.exp(d_inter - m_t)
        qk = _dot_nt(q, k) * scale * jnp.exp(d_intra - m_t)
        num = w_inter * _dot(q, c_prev.astype(BF16)) + _dot(qk.astype(BF16), v)
        den = (w_inter * jnp.sum(q.astype(F32) * n_prev, axis=-1, keepdims=True)
               + jnp.sum(qk, axis=-1, keepdims=True))
        h_tilde = num / jnp.maximum(jnp.abs(den), jnp.exp(-m_t))

        b_tot = b_r[:, L - 1:L]
        d_state = b_tot - b_c + i_c
        m_new = jnp.maximum(b_tot + m_prev, jnp.max(d_state, axis=0, keepdims=True))
        w_s = jnp.exp(d_state - m_new) * scale
        decay = jnp.exp(b_tot + m_prev - m_new)
        kw = k.astype(F32) * w_s
        c_scr[h] = decay * c_prev + _dot_tn(kw.astype(BF16), v)
        n_scr[h] = decay * n_prev + jnp.sum(kw, axis=0, keepdims=True)
        m_scr[h] = jnp.broadcast_to(m_new, (1, LANES))

        cell = jax.nn.sigmoid(o.astype(F32)) * h_tilde
        mu = jnp.mean(cell, axis=-1, keepdims=True)
        cc = cell - mu
        var = jnp.mean(cc * cc, axis=-1, keepdims=True)
        y_ref[:, cols] = (cc * lax.rsqrt(var + LN_EPS) * ng_ref[:, cols]).astype(BF16)


def _mlstm(p, gcol, grow, b_i, b_f, norm_g, b, s):
    L = MLSTM_CHUNK
    nc = s // L
    bias = jnp.concatenate([b_i, b_f]).astype(F32)
    bias_c = jnp.zeros((1, LANES), F32).at[0, :GATE_COLS].set(bias)
    bias_r = bias.reshape(GATE_COLS, 1)
    tok = lambda bi, c: (bi * nc + c, 0)
    return pl.pallas_call(
        _mlstm_body,
        grid=(b, nc),
        in_specs=[
            pl.BlockSpec((L, 4 * MLSTM_W), tok),
            pl.BlockSpec((L, LANES), tok),
            pl.BlockSpec((GATE_COLS, L), lambda bi, c: (0, bi * nc + c)),
            pl.BlockSpec((1, LANES), lambda bi, c: (0, 0)),
            pl.BlockSpec((GATE_COLS, 1), lambda bi, c: (0, 0)),
            pl.BlockSpec((1, MLSTM_W), lambda bi, c: (0, 0)),
        ],
        out_specs=pl.BlockSpec((L, MLSTM_W), tok),
        out_shape=jax.ShapeDtypeStruct((b * s, MLSTM_W), BF16),
        scratch_shapes=[
            pltpu.VMEM((MLSTM_HEADS, MLSTM_HEAD_DIM, MLSTM_HEAD_DIM), F32),
            pltpu.VMEM((MLSTM_HEADS, 1, MLSTM_HEAD_DIM), F32),
            pltpu.VMEM((MLSTM_HEADS, 1, LANES), F32),
        ],
        compiler_params=_params("parallel", "arbitrary"),
        name="mlstm",
    )(p, gcol, grow, bias_c, bias_r, norm_g.reshape(1, MLSTM_W))


def _stick_body(q_ref, k_ref, v_ref, y_ref, acc_scr, c_scr):
    t = SB_TILE
    qb = pl.program_id(2)
    row = lax.broadcasted_iota(jnp.int32, (t, t), 0)
    col = lax.broadcasted_iota(jnp.int32, (t, t), 1)
    suffix = (row >= col).astype(BF16)
    lane = lax.broadcasted_iota(jnp.int32, (t, LANES), 1)
    low = lane < SB_HEAD_DIM
    q2 = q_ref[0] * jnp.asarray(1.0 / math.sqrt(SB_HEAD_DIM), BF16)

    for sub in range(2):
        mine = low if sub == 0 else jnp.logical_not(low)
        qm = jnp.where(mine, q2, jnp.zeros_like(q2))
        acc_scr[...] = jnp.zeros_like(acc_scr)
        c_scr[...] = jnp.zeros_like(c_scr)

        def cond(carry):
            j, c_max = carry
            return (j >= 0) & (c_max > SB_LOG_FLOOR)

        def body(carry):
            j, _ = carry
            ks = pl.multiple_of(j * t, t)
            kb = k_ref[0, pl.ds(ks, t), :]
            vb = v_ref[0, pl.ds(ks, t), :]
            z = _dot_nt(qm, kb)
            before = (col < row) | (j < qb)
            lg = jnp.where(before, _neg_softplus(z), 0.0)
            incl = _split_dot_r(lg, suffix)
            c_old = c_scr[...]
            a = jnp.where(before, jnp.exp(z + incl + c_old), 0.0)
            acc_scr[...] += _dot(a.astype(BF16), vb)
            c_new = c_old + incl[:, 0:1]
            c_scr[...] = c_new
            return j - 1, jnp.max(c_new)

        lax.while_loop(cond, body, (qb, jnp.float32(0.0)))
        if sub == 0:
            first = acc_scr[...]
    y_ref[0] = jnp.where(low, first, acc_scr[...]).astype(BF16)


def _stick_breaking(p, b, s, col0):
    t = SB_TILE
    pairs = SB_W // LANES
    pv = p.reshape(b, s, p.shape[1])
    c0 = col0 // LANES
    return pl.pallas_call(
        _stick_body,
        grid=(b, pairs, s // t),
        in_specs=[
            pl.BlockSpec((1, t, LANES), lambda bi, hp, qi: (bi, qi, c0 + hp)),
            pl.BlockSpec((1, s, LANES), lambda bi, hp, qi: (bi, 0, c0 + pairs + hp)),
            pl.BlockSpec((1, s, LANES), lambda bi, hp, qi: (bi, 0, c0 + 2 * pairs + hp)),
        ],
        out_specs=pl.BlockSpec((1, t, LANES), lambda bi, hp, qi: (bi, qi, hp)),
        out_shape=jax.ShapeDtypeStruct((b, s, SB_W), BF16),
        scratch_shapes=[pltpu.VMEM((t, LANES), F32), pltpu.VMEM((t, 1), F32)],
        compiler_params=_params("parallel", "parallel", "arbitrary"),
        name="stick_breaking",
    )(pv, pv, pv).reshape(b * s, SB_W)


def _outproj_odd_body(ym_ref, ys_ref, wo_ref, x_ref, g_ref, b_ref, o_ref):
    mix = _dot(ym_ref[...], wo_ref[:MLSTM_W, :]) + _dot(ys_ref[...], wo_ref[MLSTM_W:, :])
    z = ALPHA * x_ref[...] + mix
    o_ref[...] = _layer_norm(z, g_ref[...], b_ref[...])


def _outproj_odd(y_m, y_s, w_out, x, g, b):
    n, d = x.shape
    tm = TOKEN_TILE
    return pl.pallas_call(
        _outproj_odd_body,
        grid=(n // tm,),
        in_specs=[
            pl.BlockSpec((tm, MLSTM_W), lambda i: (i, 0)),
            pl.BlockSpec((tm, SB_W), lambda i: (i, 0)),
            pl.BlockSpec((d, d), lambda i: (0, 0), pipeline_mode=pl.Buffered(1)),
            pl.BlockSpec((tm, d), lambda i: (i, 0)),
            pl.BlockSpec((1, d), lambda i: (0, 0)),
            pl.BlockSpec((1, d), lambda i: (0, 0)),
        ],
        out_specs=pl.BlockSpec((tm, d), lambda i: (i, 0)),
        out_shape=jax.ShapeDtypeStruct((n, d), F32),
        compiler_params=_params("parallel"),
        name="outproj_odd",
    )(y_m, y_s, w_out, x, g.reshape(1, d), b.reshape(1, d))


def _even_mixer(x, w_in, conv_w, w_out, g, b, batch, seq):
    pc, pq = _proj_even(x, w_in.astype(BF16))
    y_att = _dilated_attention(pq, batch, seq)
    return _outproj_even(pc, y_att, conv_w, w_out.astype(BF16), x, g, b, seq)


def _odd_mixer(x, w_in, b_i, b_f, norm_g, w_out, g, b, batch, seq):
    gate0 = 4 * MLSTM_W
    w_main = jnp.concatenate([w_in[:, :gate0], w_in[:, gate0 + GATE_COLS:]], axis=1).astype(BF16)
    w_gate = w_in[:, gate0:gate0 + GATE_COLS].astype(BF16)
    w_gate_pad = jnp.pad(w_gate, ((0, 0), (0, LANES - GATE_COLS)))
    p, gcol, grow = _proj_gates(x, w_main, w_gate_pad, w_gate.T)
    y_m = _mlstm(p, gcol, grow, b_i, b_f, norm_g, batch, seq)
    y_s = _stick_breaking(p, batch, seq, gate0)
    return _outproj_odd(y_m, y_s, w_out.astype(BF16), x, g, b)


def kernel(x, l0_ffn1_w_in, l0_ffn1_w_out, l0_ln1_g, l0_ln1_b, l0_mix_w_in, l0_conv_w, l0_mix_w_out, l0_ln2_g, l0_ln2_b, l0_ffn2_w_in, l0_ffn2_w_out, l0_ln3_g, l0_ln3_b, l1_ffn1_w_in, l1_ffn1_w_out, l1_ln1_g, l1_ln1_b, l1_mix_w_in, l1_mlstm_b_i, l1_mlstm_b_f, l1_mlstm_norm_g, l1_mix_w_out, l1_ln2_g, l1_ln2_b, l1_ffn2_w_in, l1_ffn2_w_out, l1_ln3_g, l1_ln3_b):
    batch, seq, d = x.shape
    ffn = lambda t, w_in, w_out, g, b: _ffn_ln(t, w_in.astype(BF16), w_out.astype(BF16), g, b)
    t = x.reshape(batch * seq, d)
    t = ffn(t, l0_ffn1_w_in, l0_ffn1_w_out, l0_ln1_g, l0_ln1_b)
    t = _even_mixer(t, l0_mix_w_in, l0_conv_w, l0_mix_w_out, l0_ln2_g, l0_ln2_b, batch, seq)
    t = ffn(t, l0_ffn2_w_in, l0_ffn2_w_out, l0_ln3_g, l0_ln3_b)
    t = ffn(t, l1_ffn1_w_in, l1_ffn1_w_out, l1_ln1_g, l1_ln1_b)
    t = _odd_mixer(t, l1_mix_w_in, l1_mlstm_b_i, l1_mlstm_b_f, l1_mlstm_norm_g, l1_mix_w_out,
                   l1_ln2_g, l1_ln2_b, batch, seq)
    t = ffn(t, l1_ffn2_w_in, l1_ffn2_w_out, l1_ln3_g, l1_ln3_b)
    return t.reshape(batch, seq, d)
```

```python
import functools
import math

import jax
import jax.numpy as jnp
import numpy as np
from jax import lax
from jax.experimental import pallas as pl
from jax.experimental.pallas import tpu as pltpu

F32 = jnp.float32
BF16 = jnp.bfloat16

DEPTH = 2
ALPHA = (2 * DEPTH) ** 0.25
FFN_RES_WEIGHT = 0.5
LN_EPS = 1e-5

LANES = 128
BLK = 128
CONV_CH = 256
CONV_WIDTH = 3
DSWA_HEAD_DIM = 64
DSWA_W = 768
DSWA_SPAN = 128
DSWA_PERIOD = 16
DSWA_SUPER = DSWA_PERIOD * BLK
MLSTM_HEADS = 4
MLSTM_W = 512
MLSTM_HEAD_DIM = 128
MLSTM_CHUNK = 128
SB_HEAD_DIM = 64
SB_W = 512
GATE_COLS = 2 * MLSTM_HEADS

TOKEN_TILE = 512
SB_TILE = 256
SB_LOG_FLOOR = -88.0
NEG_BIG = -1e30
VMEM_LIMIT = 56 * 1024 * 1024


def _params(*sem):
    return pltpu.CompilerParams(dimension_semantics=sem, vmem_limit_bytes=VMEM_LIMIT)


def _layer_norm(z, g, b):
    mu = jnp.mean(z, axis=-1, keepdims=True)
    zc = z - mu
    var = jnp.mean(zc * zc, axis=-1, keepdims=True)
    return zc * lax.rsqrt(var + LN_EPS) * g + b


def _dot(a, b):
    return jnp.dot(a, b, preferred_element_type=F32)


def _dot_nt(a, b):
    return lax.dot_general(a, b, (((1,), (1,)), ((), ())), preferred_element_type=F32)


def _dot_tn(a, b):
    return lax.dot_general(a, b, (((0,), (0,)), ((), ())), preferred_element_type=F32)


def _split_dot(tri, x):
    hi = x.astype(BF16)
    lo = (x - hi.astype(F32)).astype(BF16)
    return _dot(tri, hi) + _dot(tri, lo)


def _split_dot_r(x, tri):
    hi = x.astype(BF16)
    lo = (x - hi.astype(F32)).astype(BF16)
    return _dot(hi, tri) + _dot(lo, tri)


def _neg_softplus(z):
    return -(jnp.maximum(z, 0.0) + jnp.log(1.0 + jnp.exp(-jnp.abs(z))))


def _ffn_ln_body(x_ref, win_ref, wout_ref, g_ref, b_ref, o_ref, *, d_ff, tf):
    x = x_ref[...]
    xb = x.astype(BF16)
    acc = None
    for c in range(d_ff // tf):
        gate = _dot(xb, win_ref[:, c * tf:(c + 1) * tf])
        up = _dot(xb, win_ref[:, d_ff + c * tf:d_ff + (c + 1) * tf])
        act = (gate * jax.nn.sigmoid(gate) * up).astype(BF16)
        part = _dot(act, wout_ref[c * tf:(c + 1) * tf, :])
        acc = part if acc is None else acc + part
    z = ALPHA * x + FFN_RES_WEIGHT * acc
    o_ref[...] = _layer_norm(z, g_ref[...], b_ref[...])


def _ffn_ln(x, w_in, w_out, g, b):
    n, d = x.shape
    d_ff = w_out.shape[0]
    tf = d_ff // 2
    tm = TOKEN_TILE
    const = dict(pipeline_mode=pl.Buffered(1))
    return pl.pallas_call(
        functools.partial(_ffn_ln_body, d_ff=d_ff, tf=tf),
        grid=(n // tm,),
        in_specs=[
            pl.BlockSpec((tm, d), lambda i: (i, 0)),
            pl.BlockSpec((d, 2 * d_ff), lambda i: (0, 0), **const),
            pl.BlockSpec((d_ff, d), lambda i: (0, 0), **const),
            pl.BlockSpec((1, d), lambda i: (0, 0)),
            pl.BlockSpec((1, d), lambda i: (0, 0)),
        ],
        out_specs=pl.BlockSpec((tm, d), lambda i: (i, 0)),
        out_shape=jax.ShapeDtypeStruct((n, d), F32),
        compiler_params=_params("parallel"),
        name="ffn_ln",
    )(x, w_in, w_out, g.reshape(1, d), b.reshape(1, d))


def _proj_even_body(x_ref, w_ref, pc_ref, pq_ref, slab, *, tm, tn):
    cw = 3 * CONV_CH
    xb = x_ref[...].astype(BF16)
    pc_ref[...] = _dot(xb, w_ref[:, :cw]).astype(BF16)
    per = tn // LANES
    for c in range((w_ref.shape[1] - cw) // tn):
        res = _dot(xb, w_ref[:, cw + c * tn:cw + (c + 1) * tn])
        for j in range(per):
            slab[c * per + j] = res[:, j * LANES:(j + 1) * LANES]
    rows = tm // DSWA_PERIOD
    for r in range(DSWA_PERIOD):
        for c in range(slab.shape[0]):
            piece = slab[c, pl.ds(r, rows, stride=DSWA_PERIOD), :]
            pq_ref[0, r, :, c * LANES:(c + 1) * LANES] = piece.astype(BF16)


def _proj_even(x, w):
    n, d = x.shape
    cw = 3 * CONV_CH
    qw = w.shape[1] - cw
    tm = TOKEN_TILE
    tiles = DSWA_SUPER // tm
    rows = tm // DSWA_PERIOD
    return pl.pallas_call(
        functools.partial(_proj_even_body, tm=tm, tn=256),
        grid=(n // tm,),
        in_specs=[
            pl.BlockSpec((tm, d), lambda i: (i, 0)),
            pl.BlockSpec((d, cw + qw), lambda i: (0, 0), pipeline_mode=pl.Buffered(1)),
        ],
        out_specs=[
            pl.BlockSpec((tm, cw), lambda i: (i, 0)),
            pl.BlockSpec((1, DSWA_PERIOD, rows, qw), lambda i: (i // tiles, 0, i % tiles, 0)),
        ],
        out_shape=[
            jax.ShapeDtypeStruct((n, cw), BF16),
            jax.ShapeDtypeStruct((n // DSWA_SUPER, DSWA_PERIOD, DSWA_SUPER // DSWA_PERIOD, qw), BF16),
        ],
        scratch_shapes=[pltpu.VMEM((qw // LANES, tm, LANES), F32)],
        compiler_params=_params("parallel"),
        name="proj_even",
    )(x, w)


def _proj_gates_body(x_ref, w_ref, wg_ref, wgt_ref, o_ref, gc_ref, gr_ref, *, tn):
    xb = x_ref[...].astype(BF16)
    for c in range(w_ref.shape[1] // tn):
        o_ref[:, c * tn:(c + 1) * tn] = _dot(xb, w_ref[:, c * tn:(c + 1) * tn]).astype(BF16)
    gc_ref[...] = _dot(xb, wg_ref[...])
    gr_ref[...] = _dot_nt(wgt_ref[...], xb)


def _proj_gates(x, w, wg, wgt):
    n, d = x.shape
    p = w.shape[1]
    tm = TOKEN_TILE
    const = dict(pipeline_mode=pl.Buffered(1))
    return pl.pallas_call(
        functools.partial(_proj_gates_body, tn=512),
        grid=(n // tm,),
        in_specs=[
            pl.BlockSpec((tm, d), lambda i: (i, 0)),
            pl.BlockSpec((d, p), lambda i: (0, 0), **const),
            pl.BlockSpec((d, LANES), lambda i: (0, 0), **const),
            pl.BlockSpec((GATE_COLS, d), lambda i: (0, 0), **const),
        ],
        out_specs=[
            pl.BlockSpec((tm, p), lambda i: (i, 0)),
            pl.BlockSpec((tm, LANES), lambda i: (i, 0)),
            pl.BlockSpec((GATE_COLS, tm), lambda i: (0, i)),
        ],
        out_shape=[
            jax.ShapeDtypeStruct((n, p), BF16),
            jax.ShapeDtypeStruct((n, LANES), F32),
            jax.ShapeDtypeStruct((GATE_COLS, n), F32),
        ],
        compiler_params=_params("parallel"),
        name="proj_odd",
    )(x, w, wg, wgt)


def _dilated_biases():
    def make(q_pos, k_pos, k_is_prev):
        rel = q_pos[:, None] - k_pos[None, :]
        ok = (rel >= 0) & (rel <= DSWA_SPAN)
        variants = [ok, ok & ~k_is_prev[None, :]]
        return jnp.asarray(np.stack([np.where(v, 0.0, NEG_BIG) for v in variants]), F32)

    per = DSWA_PERIOD
    kj = np.arange(2 * BLK)
    b16 = make(np.arange(BLK), kj - BLK, kj < BLK)
    qa, qi = np.divmod(np.arange(BLK), BLK // 4)
    ka, ki = np.divmod(np.arange(2 * BLK), 2 * BLK // 4)
    b4 = make(4 * qi + qa, 4 * (ki - BLK // 4) + ka, ki < BLK // 4)
    qg, qi = np.divmod(np.arange(2 * BLK), 2 * BLK // per)
    kg, ki = np.divmod(np.arange(4 * BLK), 4 * BLK // per)
    b1 = make(per * qi + qg, per * (ki - 2 * BLK // per) + kg, ki < 2 * BLK // per)
    return b16, b4, b1


def _dilated_body(b16_ref, b4_ref, b1_ref, q_ref, kp_ref, kc_ref, vp_ref, vc_ref, y_ref,
                  kf, vf, acc, m_s, l_s, nat):
    per = DSWA_PERIOD
    first = jnp.where(pl.program_id(1) == 0, 1, 0)
    kf[:, :BLK, :] = kp_ref[0]
    kf[:, BLK:, :] = kc_ref[0]
    vf[:, :BLK, :] = vp_ref[0]
    vf[:, BLK:, :] = vc_ref[0]
    scale = jnp.asarray(1.0 / math.sqrt(DSWA_HEAD_DIM), BF16)

    def low_mask(rows):
        return lax.broadcasted_iota(jnp.int32, (rows, LANES), 1) < DSWA_HEAD_DIM

    def attend(units):
        scores = []
        for q2, kk, _, _ in units:
            low = low_mask(q2.shape[0])
            zero = jnp.zeros_like(q2)
            stacked = jnp.concatenate([jnp.where(low, q2, zero), jnp.where(low, zero, q2)], axis=0)
            scores.append(_dot_nt(stacked, kk))
        soft = []
        for s, (_, _, _, bias) in zip(scores, units):
            s = s + jnp.concatenate([bias, bias], axis=0)
            m = jnp.max(s, axis=-1, keepdims=True)
            p = jnp.exp(s - m)
            soft.append((m, jnp.sum(p, axis=-1, keepdims=True), p.astype(BF16)))
        return [(m, l, _dot(p, vv)) for (m, l, p), (_, _, vv, _) in zip(soft, units)]

    def merge(part, m_old, l_old, acc_old):
        m, l, pv = part
        r = acc_old.shape[0]
        low = low_mask(r)
        m_tot = jnp.maximum(m_old, m)
        a_old = jnp.exp(m_old - m_tot)
        a_new = jnp.exp(m - m_tot)
        fresh = pv * a_new
        acc_new = acc_old * jnp.where(low, a_old[:r], a_old[r:]) + jnp.where(low, fresh[:r], fresh[r:])
        return m_tot, l_old * a_old + l * a_new, acc_new

    def gather(ref, lead, groups, rows):
        return jnp.concatenate([ref[lead + (g, rows, slice(None))] for g in groups], axis=0)

    def gather_heads(ref, groups, rows):
        return jnp.concatenate([gather(ref, (sub,), groups, rows) for sub in range(2)], axis=0)

    def scatter(ref, lead, groups, rows, value):
        n = value.shape[0] // len(groups)
        for j, g in enumerate(groups):
            ref[lead + (g, rows, slice(None))] = value[j * n:(j + 1) * n]

    def scatter_heads(ref, groups, rows, value):
        r = value.shape[0] // 2
        for sub in range(2):
            scatter(ref, (sub,), groups, rows, value[sub * r:(sub + 1) * r])

    def fold(parts, where):
        merged = []
        for part, (groups, rows) in zip(parts, where):
            m, l, a = merge(part, gather_heads(m_s, groups, rows), gather_heads(l_s, groups, rows),
                            gather(acc, (), groups, rows))
            merged.append((m, l, a, groups, rows))
        return merged

    full = slice(None)

    def d16(it, carry):
        groups = [4 * it + j for j in range(4)]
        parts = attend([(q_ref[0, g] * scale, kf[g], vf[g], b16_ref[first]) for g in groups])
        for g, (m, l, pv) in zip(groups, parts):
            scatter_heads(m_s, [g], full, jnp.broadcast_to(m, (2 * BLK, LANES)))
            scatter_heads(l_s, [g], full, jnp.broadcast_to(l, (2 * BLK, LANES)))
            acc[g] = jnp.where(low_mask(BLK), pv[:BLK], pv[BLK:])
        return carry

    lax.fori_loop(0, per // 4, d16, 0)

    def d4(r4, carry):
        nq = BLK // 4
        groups = [4 * a + r4 for a in range(4)]
        units, where = [], []
        for blk in range(4):
            qrows = slice(blk * nq, (blk + 1) * nq)
            krows = slice(BLK - nq + blk * nq, BLK + (blk + 1) * nq)
            bias = b4_ref[first] if blk == 0 else b4_ref[0]
            units.append((gather(q_ref, (0,), groups, qrows) * scale,
                          gather(kf, (), groups, krows), gather(vf, (), groups, krows), bias))
            where.append((groups, qrows))
        for m, l, a, groups, rows in fold(attend(units), where):
            scatter_heads(m_s, groups, rows, m)
            scatter_heads(l_s, groups, rows, l)
            scatter(acc, (), groups, rows, a)
        return carry

    lax.fori_loop(0, 4, d4, 0)

    def d1(it, carry):
        nq = 2 * BLK // per
        groups = list(range(per))
        units, where, blks = [], [], []
        for j in range(2):
            blk = 2 * it + j
            qrows = pl.ds(pl.multiple_of(blk * nq, nq), nq)
            krows = pl.ds(pl.multiple_of(BLK - nq + blk * nq, nq), 2 * nq)
            flag = jnp.where(blk == 0, first, 0)
            units.append((gather(q_ref, (0,), groups, qrows) * scale,
                          gather(kf, (), groups, krows), gather(vf, (), groups, krows), b1_ref[flag]))
            where.append((groups, qrows))
            blks.append(blk)
        for blk, (m, l, a, _, _) in zip(blks, fold(attend(units), where)):
            r = a.shape[0]
            out = a / jnp.where(low_mask(r), l[:r], l[r:])
            for g in groups:
                nat[pl.ds(g, nq, stride=per), :] = out[g * nq:(g + 1) * nq]
            y_ref[0, pl.ds(pl.multiple_of(blk * r, r), r), :] = nat[...].astype(BF16)
        return carry

    lax.fori_loop(0, DSWA_SUPER // (4 * BLK), d1, 0)


def _dilated_attention(pq, b, s):
    per, w = DSWA_PERIOD, DSWA_W
    pairs = w // LANES
    nsb = s // DSWA_SUPER
    rows = DSWA_SUPER // per
    biases = _dilated_biases()
    blk = (1, per, rows, LANES)
    cur = lambda col: (lambda bi, sb, hp: (bi * nsb + sb, 0, 0, col * pairs + hp))
    prev = lambda col: (lambda bi, sb, hp: (bi * nsb + jnp.maximum(sb - 1, 0), 0, 0, col * pairs + hp))
    const = lambda a: pl.BlockSpec(a.shape, lambda bi, sb, hp: (0, 0, 0))
    return pl.pallas_call(
        _dilated_body,
        grid=(b, nsb, pairs),
        in_specs=[const(a) for a in biases] + [
            pl.BlockSpec(blk, cur(0)),
            pl.BlockSpec(blk, prev(1)),
            pl.BlockSpec(blk, cur(1)),
            pl.BlockSpec(blk, prev(2)),
            pl.BlockSpec(blk, cur(2)),
        ],
        out_specs=pl.BlockSpec((1, DSWA_SUPER, LANES), lambda bi, sb, hp: (bi * nsb + sb, 0, hp)),
        out_shape=jax.ShapeDtypeStruct((b * nsb, DSWA_SUPER, w), BF16),
        scratch_shapes=[
            pltpu.VMEM((per, 2 * rows, LANES), BF16),
            pltpu.VMEM((per, 2 * rows, LANES), BF16),
            pltpu.VMEM((per, rows, LANES), F32),
            pltpu.VMEM((2, per, rows, LANES), F32),
            pltpu.VMEM((2, per, rows, LANES), F32),
            pltpu.VMEM((2 * BLK, LANES), F32),
        ],
        compiler_params=_params("parallel", "parallel", "parallel"),
        name="dilated",
    )(*biases, pq, pq, pq, pq, pq).reshape(b * s, w)


def _outproj_even_body(p_ref, halo_ref, y_ref, cw_ref, wo_ref, x_ref, g_ref, b_ref, o_ref, *, seq, tm, halo):
    c = CONV_CH
    pc = p_ref[...].astype(F32)
    bg, u = pc[:, :c], pc[:, c:2 * c] * pc[:, 2 * c:3 * c]
    ph = halo_ref[...].astype(F32)
    uh = ph[:, c:2 * c] * ph[:, 2 * c:3 * c]
    starts_sequence = (pl.program_id(0) * tm) % seq == 0
    uh = jnp.where(starts_sequence, 0.0, uh)
    full = jnp.concatenate([uh, u], axis=0)
    cw = cw_ref[...]
    conv = cw[2:3, :] * u + cw[1:2, :] * full[halo - 1:halo - 1 + tm] + cw[0:1, :] * full[halo - 2:halo - 2 + tm]
    y_conv = (bg * conv).astype(BF16)
    mix = _dot(y_conv, wo_ref[:c, :]) + _dot(y_ref[...], wo_ref[c:, :])
    z = ALPHA * x_ref[...] + mix
    o_ref[...] = _layer_norm(z, g_ref[...], b_ref[...])


def _outproj_even(p, y_att, conv_w, w_out, x, g, b, seq):
    n, d = x.shape
    tm = TOKEN_TILE
    halo = 16
    cw = 3 * CONV_CH
    return pl.pallas_call(
        functools.partial(_outproj_even_body, seq=seq, tm=tm, halo=halo),
        grid=(n // tm,),
        in_specs=[
            pl.BlockSpec((tm, cw), lambda i: (i, 0)),
            pl.BlockSpec((halo, cw), lambda i: (jnp.maximum(i * (tm // halo) - 1, 0), 0)),
            pl.BlockSpec((tm, DSWA_W), lambda i: (i, 0)),
            pl.BlockSpec((CONV_WIDTH, CONV_CH), lambda i: (0, 0)),
            pl.BlockSpec((d, d), lambda i: (0, 0), pipeline_mode=pl.Buffered(1)),
            pl.BlockSpec((tm, d), lambda i: (i, 0)),
            pl.BlockSpec((1, d), lambda i: (0, 0)),
            pl.BlockSpec((1, d), lambda i: (0, 0)),
        ],
        out_specs=pl.BlockSpec((tm, d), lambda i: (i, 0)),
        out_shape=jax.ShapeDtypeStruct((n, d), F32),
        compiler_params=_params("parallel"),
        name="outproj_even",
    )(p, p, y_att, conv_w, w_out, x, g.reshape(1, d), b.reshape(1, d))


def _mlstm_body(p_ref, gc_ref, gr_ref, bc_ref, br_ref, ng_ref, y_ref, c_scr, n_scr, m_scr):
    L, dh, nh = MLSTM_CHUNK, MLSTM_HEAD_DIM, MLSTM_HEADS

    @pl.when(pl.program_id(1) == 0)
    def _():
        c_scr[...] = jnp.zeros_like(c_scr)
        n_scr[...] = jnp.zeros_like(n_scr)
        m_scr[...] = jnp.zeros_like(m_scr)

    row = lax.broadcasted_iota(jnp.int32, (L, L), 0)
    col = lax.broadcasted_iota(jnp.int32, (L, L), 1)
    causal = col <= row
    tri = causal.astype(BF16)
    tri_t = (row <= col).astype(BF16)

    gcol = gc_ref[...] + bc_ref[...]
    grow = gr_ref[...] + br_ref[...]
    bcum_c = _split_dot(tri, _neg_softplus(-gcol))
    bcum_r = _split_dot_r(_neg_softplus(-grow), tri_t)
    scale = 1.0 / math.sqrt(dh)

    heads = range(nh)
    part = lambda j, h: p_ref[:, j * MLSTM_W + h * dh:j * MLSTM_W + (h + 1) * dh]
    q, k, v = ([part(j, h) for h in heads] for j in range(3))
    rep = lambda a: jnp.broadcast_to(a, (L, LANES))
    i_c = [rep(gcol[:, h:h + 1]) for h in heads]
    b_c = [rep(bcum_c[:, nh + h:nh + h + 1]) for h in heads]
    i_r = [grow[h:h + 1, :] for h in heads]
    b_r = [bcum_r[nh + h:nh + h + 1, :] for h in heads]
    m_prev = [m_scr[h] for h in heads]
    c_prev = [c_scr[h] for h in heads]
    n_prev = [n_scr[h] for h in heads]

    s = [_dot_nt(q[h], k[h]) for h in heads]
    q_c = [_dot(q[h], c_prev[h].astype(BF16)) for h in heads]

    for h in heads:
        b_tot = b_c[h][L - 1:L, :]
        d_state = b_tot - b_c[h] + i_c[h]
        m_new = jnp.maximum(b_tot + m_prev[h], jnp.max(d_state, axis=0, keepdims=True))
        kw = k[h].astype(F32) * (jnp.exp(d_state - m_new) * scale)
        decay = jnp.exp(b_tot + m_prev[h] - m_new)
        c_scr[h] = decay * c_prev[h] + _dot_tn(kw.astype(BF16), v[h])
        n_scr[h] = decay * n_prev[h] + jnp.sum(kw, axis=0, keepdims=True)
        m_scr[h] = m_new

    w_inter, m_t, qk = [], [], []
    for h in heads:
        d_intra = jnp.where(causal, b_c[h] - b_r[h] + i_r[h], NEG_BIG)
        d_inter = b_c[h] + m_prev[h]
        m_t.append(jnp.maximum(d_inter, jnp.max(d_intra, axis=-1, keepdims=True)))
        w_inter.append(jnp.exp(d_inter - m_t[h]))
        qk.append(s[h] * scale * jnp.exp(d_intra - m_t[h]))
    qk_v = [_dot(qk[h].astype(BF16), v[h]) for h in heads]

    for h in heads:
        cols = slice(h * dh, (h + 1) * dh)
        num = w_inter[h] * q_c[h] + qk_v[h]
        den = (w_inter[h] * jnp.sum(q[h].astype(F32) * n_prev[h], axis=-1, keepdims=True)
               + jnp.sum(qk[h], axis=-1, keepdims=True))
        h_tilde = num / jnp.maximum(jnp.abs(den), jnp.exp(-m_t[h]))
        cell = jax.nn.sigmoid(part(3, h).astype(F32)) * h_tilde
        mu = jnp.mean(cell, axis=-1, keepdims=True)
        cc = cell - mu
        var = jnp.mean(cc * cc, axis=-1, keepdims=True)
        y_ref[:, cols] = (cc * lax.rsqrt(var + LN_EPS) * ng_ref[:, cols]).astype(BF16)


def _mlstm(p, gcol, grow, b_i, b_f, norm_g, b, s):
    L = MLSTM_CHUNK
    nc = s // L
    bias = jnp.concatenate([b_i, b_f]).astype(F32)
    bias_c = jnp.zeros((1, LANES), F32).at[0, :GATE_COLS].set(bias)
    bias_r = bias.reshape(GATE_COLS, 1)
    tok = lambda bi, c: (bi * nc + c, 0)
    return pl.pallas_call(
        _mlstm_body,
        grid=(b, nc),
        in_specs=[
            pl.BlockSpec((L, 4 * MLSTM_W), tok),
            pl.BlockSpec((L, LANES), tok),
            pl.BlockSpec((GATE_COLS, L), lambda bi, c: (0, bi * nc + c)),
            pl.BlockSpec((1, LANES), lambda bi, c: (0, 0)),
            pl.BlockSpec((GATE_COLS, 1), lambda bi, c: (0, 0)),
            pl.BlockSpec((1, MLSTM_W), lambda bi, c: (0, 0)),
        ],
        out_specs=pl.BlockSpec((L, MLSTM_W), tok),
        out_shape=jax.ShapeDtypeStruct((b * s, MLSTM_W), BF16),
        scratch_shapes=[
            pltpu.VMEM((MLSTM_HEADS, MLSTM_HEAD_DIM, MLSTM_HEAD_DIM), F32),
            pltpu.VMEM((MLSTM_HEADS, 1, MLSTM_HEAD_DIM), F32),
            pltpu.VMEM((MLSTM_HEADS, 1, LANES), F32),
        ],
        compiler_params=_params("parallel", "arbitrary"),
        name="mlstm",
    )(p, gcol, grow, bias_c, bias_r, norm_g.reshape(1, MLSTM_W))


def _stick_body(bias_ref, q_ref, k_ref, v_ref, y_ref, acc_scr, c_scr):
    t = SB_TILE
    qb = pl.program_id(2)
    row = lax.broadcasted_iota(jnp.int32, (t, t), 0)
    col = lax.broadcasted_iota(jnp.int32, (t, t), 1)
    suffix = (row >= col).astype(BF16)
    low = lax.broadcasted_iota(jnp.int32, (t, LANES), 1) < SB_HEAD_DIM
    q2 = q_ref[0] * jnp.asarray(1.0 / math.sqrt(SB_HEAD_DIM), BF16)
    zero = jnp.zeros_like(q2)
    qs = jnp.concatenate([jnp.where(low, q2, zero), jnp.where(low, zero, q2)], axis=0)

    def keys(j):
        return pl.ds(pl.multiple_of(j * t, t), t)

    def block(z, c_later):
        lg = _neg_softplus(z)
        incl = _split_dot_r(lg, suffix)
        return jnp.exp(z + incl + c_later).astype(BF16), jnp.sum(lg, axis=-1, keepdims=True)

    j_prev = jnp.maximum(qb - 1, 0)
    causal = bias_ref[...]
    z_diag = _dot_nt(qs, k_ref[0, keys(qb), :]) + jnp.concatenate([causal, causal], axis=0)
    z_prev = _dot_nt(qs, k_ref[0, keys(j_prev), :]) + jnp.where(qb > 0, 0.0, NEG_BIG)
    a_diag, c_diag = block(z_diag, 0.0)
    a_prev, c_prev = block(z_prev, c_diag)
    c_both = c_diag + c_prev
    acc_scr[...] = _dot(a_diag, v_ref[0, keys(qb), :]) + _dot(a_prev, v_ref[0, keys(j_prev), :])
    c_scr[...] = jnp.broadcast_to(c_both, c_scr.shape)

    def cond(carry):
        j, c_max = carry
        return (j >= 0) & (c_max > SB_LOG_FLOOR)

    def body(carry):
        j, _ = carry
        c_old = c_scr[...]
        a, c_blk = block(_dot_nt(qs, k_ref[0, keys(j), :]), jnp.concatenate([c_old] * (t // LANES), axis=1))
        acc_scr[...] += _dot(a, v_ref[0, keys(j), :])
        c_new = c_old + c_blk
        c_scr[...] = c_new
        return j - 1, jnp.max(c_new)

    lax.while_loop(cond, body, (qb - 2, jnp.max(c_both)))
    out = acc_scr[...]
    y_ref[0] = jnp.where(low, out[:t], out[t:]).astype(BF16)


def _stick_breaking(p, b, s, col0):
    t = SB_TILE
    pairs = SB_W // LANES
    pv = p.reshape(b, s, p.shape[1])
    c0 = col0 // LANES
    idx = np.arange(t)
    causal = jnp.asarray(np.where(idx[None, :] < idx[:, None], 0.0, NEG_BIG), F32)
    return pl.pallas_call(
        _stick_body,
        grid=(b, pairs, s // t),
        in_specs=[
            pl.BlockSpec((t, t), lambda bi, hp, qi: (0, 0)),
            pl.BlockSpec((1, t, LANES), lambda bi, hp, qi: (bi, qi, c0 + hp)),
            pl.BlockSpec((1, s, LANES), lambda bi, hp, qi: (bi, 0, c0 + pairs + hp)),
            pl.BlockSpec((1, s, LANES), lambda bi, hp, qi: (bi, 0, c0 + 2 * pairs + hp)),
        ],
        out_specs=pl.BlockSpec((1, t, LANES), lambda bi, hp, qi: (bi, qi, hp)),
        out_shape=jax.ShapeDtypeStruct((b, s, SB_W), BF16),
        scratch_shapes=[pltpu.VMEM((2 * t, LANES), F32), pltpu.VMEM((2 * t, LANES), F32)],
        compiler_params=_params("parallel", "parallel", "arbitrary"),
        name="stick_breaking",
    )(causal, pv, pv, pv).reshape(b * s, SB_W)


def _outproj_odd_body(ym_ref, ys_ref, wo_ref, x_ref, g_ref, b_ref, o_ref):
    mix = _dot(ym_ref[...], wo_ref[:MLSTM_W, :]) + _dot(ys_ref[...], wo_ref[MLSTM_W:, :])
    z = ALPHA * x_ref[...] + mix
    o_ref[...] = _layer_norm(z, g_ref[...], b_ref[...])


def _outproj_odd(y_m, y_s, w_out, x, g, b):
    n, d = x.shape
    tm = TOKEN_TILE
    return pl.pallas_call(
        _outproj_odd_body,
        grid=(n // tm,),
        in_specs=[
            pl.BlockSpec((tm, MLSTM_W), lambda i: (i, 0)),
            pl.BlockSpec((tm, SB_W), lambda i: (i, 0)),
            pl.BlockSpec((d, d), lambda i: (0, 0), pipeline_mode=pl.Buffered(1)),
            pl.BlockSpec((tm, d), lambda i: (i, 0)),
            pl.BlockSpec((1, d), lambda i: (0, 0)),
            pl.BlockSpec((1, d), lambda i: (0, 0)),
        ],
        out_specs=pl.BlockSpec((tm, d), lambda i: (i, 0)),
        out_shape=jax.ShapeDtypeStruct((n, d), F32),
        compiler_params=_params("parallel"),
        name="outproj_odd",
    )(y_m, y_s, w_out, x, g.reshape(1, d), b.reshape(1, d))


def _even_mixer(x, w_in, conv_w, w_out, g, b, batch, seq):
    pc, pq = _proj_even(x, w_in.astype(BF16))
    y_att = _dilated_attention(pq, batch, seq)
    return _outproj_even(pc, y_att, conv_w, w_out.astype(BF16), x, g, b, seq)


def _odd_mixer(x, w_in, b_i, b_f, norm_g, w_out, g, b, batch, seq):
    gate0 = 4 * MLSTM_W
    w_main = jnp.concatenate([w_in[:, :gate0], w_in[:, gate0 + GATE_COLS:]], axis=1).astype(BF16)
    w_gate = w_in[:, gate0:gate0 + GATE_COLS].astype(BF16)
    w_gate_pad = jnp.pad(w_gate, ((0, 0), (0, LANES - GATE_COLS)))
    p, gcol, grow = _proj_gates(x, w_main, w_gate_pad, w_gate.T)
    y_m = _mlstm(p, gcol, grow, b_i, b_f, norm_g, batch, seq)
    y_s = _stick_breaking(p, batch, seq, gate0)
    return _outproj_odd(y_m, y_s, w_out.astype(BF16), x, g, b)


def kernel(x, l0_ffn1_w_in, l0_ffn1_w_out, l0_ln1_g, l0_ln1_b, l0_mix_w_in, l0_conv_w, l0_mix_w_out, l0_ln2_g, l0_ln2_b, l0_ffn2_w_in, l0_ffn2_w_out, l0_ln3_g, l0_ln3_b, l1_ffn1_w_in, l1_ffn1_w_out, l1_ln1_g, l1_ln1_b, l1_mix_w_in, l1_mlstm_b_i, l1_mlstm_b_f, l1_mlstm_norm_g, l1_mix_w_out, l1_ln2_g, l1_ln2_b, l1_ffn2_w_in, l1_ffn2_w_out, l1_ln3_g, l1_ln3_b):
    batch, seq, d = x.shape
    ffn = lambda t, w_in, w_out, g, b: _ffn_ln(t, w_in.astype(BF16), w_out.astype(BF16), g, b)
    t = x.reshape(batch * seq, d)
    t = ffn(t, l0_ffn1_w_in, l0_ffn1_w_out, l0_ln1_g, l0_ln1_b)
    t = _even_mixer(t, l0_mix_w_in, l0_conv_w, l0_mix_w_out, l0_ln2_g, l0_ln2_b, batch, seq)
    t = ffn(t, l0_ffn2_w_in, l0_ffn2_w_out, l0_ln3_g, l0_ln3_b)
    t = ffn(t, l1_ffn1_w_in, l1_ffn1_w_out, l1_ln1_g, l1_ln1_b)
    t = _odd_mixer(t, l1_mix_w_in, l1_mlstm_b_i, l1_mlstm_b_f, l1_mlstm_norm_g, l1_mix_w_out,
                   l1_ln2_g, l1_ln2_b, batch, seq)
    t = ffn(t, l1_ffn2_w_in, l1_ffn2_w_out, l1_ln3_g, l1_ln3_b)
    return t.reshape(batch, seq, d)
```

```python
import functools
import math

import jax
import jax.numpy as jnp
import numpy as np
from jax import lax
from jax.experimental import pallas as pl
from jax.experimental.pallas import tpu as pltpu

F32 = jnp.float32
BF16 = jnp.bfloat16

DEPTH = 2
ALPHA = (2 * DEPTH) ** 0.25
FFN_RES_WEIGHT = 0.5
LN_EPS = 1e-5

LANES = 128
BLK = 128
CONV_CH = 256
CONV_WIDTH = 3
DSWA_HEAD_DIM = 64
DSWA_W = 768
DSWA_SPAN = 128
DSWA_PERIOD = 16
DSWA_SUPER = DSWA_PERIOD * BLK
MLSTM_HEADS = 4
MLSTM_W = 512
MLSTM_HEAD_DIM = 128
MLSTM_CHUNK = 128
SB_HEAD_DIM = 64
SB_W = 512
GATE_COLS = 2 * MLSTM_HEADS

TOKEN_TILE = 512
SB_TILE = 256
SB_UNITS = 2
SB_LOG2_FLOOR = -127.0
LOG2E = math.log2(math.e)
NEG_BIG = -1e30
VMEM_LIMIT = 56 * 1024 * 1024


def _params(*sem):
    return pltpu.CompilerParams(dimension_semantics=sem, vmem_limit_bytes=VMEM_LIMIT)


def _layer_norm(z, g, b):
    mu = jnp.mean(z, axis=-1, keepdims=True)
    zc = z - mu
    var = jnp.mean(zc * zc, axis=-1, keepdims=True)
    return zc * lax.rsqrt(var + LN_EPS) * g + b


def _dot(a, b):
    return jnp.dot(a, b, preferred_element_type=F32)


def _dot_nt(a, b):
    return lax.dot_general(a, b, (((1,), (1,)), ((), ())), preferred_element_type=F32)


def _dot_tn(a, b):
    return lax.dot_general(a, b, (((0,), (0,)), ((), ())), preferred_element_type=F32)


def _split_dot(tri, x):
    hi = x.astype(BF16)
    lo = (x - hi.astype(F32)).astype(BF16)
    return _dot(tri, hi) + _dot(tri, lo)


def _split_dot_r(x, tri):
    hi = x.astype(BF16)
    lo = (x - hi.astype(F32)).astype(BF16)
    return _dot(hi, tri) + _dot(lo, tri)


def _neg_softplus(z):
    return -(jnp.maximum(z, 0.0) + jnp.log(1.0 + jnp.exp(-jnp.abs(z))))


def _neg_softplus2(z2):
    return -(jnp.maximum(z2, 0.0) + jnp.log(1.0 + jnp.exp2(-jnp.abs(z2))) * LOG2E)


def _scaled_q(q, scale):
    return (q.astype(F32) * scale).astype(BF16)


def _ffn_ln_body(x_ref, win_ref, wout_ref, g_ref, b_ref, o_ref, *, d_ff, tf):
    x = x_ref[...]
    xb = x.astype(BF16)
    acc = None
    for c in range(d_ff // tf):
        gate = _dot(xb, win_ref[:, c * tf:(c + 1) * tf])
        up = _dot(xb, win_ref[:, d_ff + c * tf:d_ff + (c + 1) * tf])
        act = (gate * jax.nn.sigmoid(gate) * up).astype(BF16)
        part = _dot(act, wout_ref[c * tf:(c + 1) * tf, :])
        acc = part if acc is None else acc + part
    z = ALPHA * x + FFN_RES_WEIGHT * acc
    o_ref[...] = _layer_norm(z, g_ref[...], b_ref[...])


def _ffn_ln(x, w_in, w_out, g, b):
    n, d = x.shape
    d_ff = w_out.shape[0]
    tf = d_ff // 2
    tm = TOKEN_TILE
    const = dict(pipeline_mode=pl.Buffered(1))
    return pl.pallas_call(
        functools.partial(_ffn_ln_body, d_ff=d_ff, tf=tf),
        grid=(n // tm,),
        in_specs=[
            pl.BlockSpec((tm, d), lambda i: (i, 0)),
            pl.BlockSpec((d, 2 * d_ff), lambda i: (0, 0), **const),
            pl.BlockSpec((d_ff, d), lambda i: (0, 0), **const),
            pl.BlockSpec((1, d), lambda i: (0, 0)),
            pl.BlockSpec((1, d), lambda i: (0, 0)),
        ],
        out_specs=pl.BlockSpec((tm, d), lambda i: (i, 0)),
        out_shape=jax.ShapeDtypeStruct((n, d), F32),
        compiler_params=_params("parallel"),
        name="ffn_ln",
    )(x, w_in, w_out, g.reshape(1, d), b.reshape(1, d))


def _proj_even_body(x_ref, perm_ref, w_ref, pc_ref, pq_ref, *, tm, tn):
    cw = 3 * CONV_CH
    xb = x_ref[...].astype(BF16)
    pc_ref[...] = _dot(xb, w_ref[:, :cw]).astype(BF16)
    xg = _dot(perm_ref[...], xb).astype(BF16)
    rows = tm // DSWA_PERIOD
    for c in range((w_ref.shape[1] - cw) // tn):
        res = _dot(xg, w_ref[:, cw + c * tn:cw + (c + 1) * tn]).astype(BF16)
        pq_ref[0, :, :, c * tn:(c + 1) * tn] = res.reshape(DSWA_PERIOD, rows, tn)


def _proj_even(x, w):
    n, d = x.shape
    cw = 3 * CONV_CH
    qw = w.shape[1] - cw
    tm = TOKEN_TILE
    tiles = DSWA_SUPER // tm
    rows = tm // DSWA_PERIOD
    dst = np.arange(tm)
    perm = np.zeros((tm, tm), np.float32)
    perm[dst, (dst % rows) * DSWA_PERIOD + dst // rows] = 1.0
    const = dict(pipeline_mode=pl.Buffered(1))
    return pl.pallas_call(
        functools.partial(_proj_even_body, tm=tm, tn=768),
        grid=(n // tm,),
        in_specs=[
            pl.BlockSpec((tm, d), lambda i: (i, 0)),
            pl.BlockSpec((tm, tm), lambda i: (0, 0), **const),
            pl.BlockSpec((d, cw + qw), lambda i: (0, 0), **const),
        ],
        out_specs=[
            pl.BlockSpec((tm, cw), lambda i: (i, 0)),
            pl.BlockSpec((1, DSWA_PERIOD, rows, qw), lambda i: (i // tiles, 0, i % tiles, 0)),
        ],
        out_shape=[
            jax.ShapeDtypeStruct((n, cw), BF16),
            jax.ShapeDtypeStruct((n // DSWA_SUPER, DSWA_PERIOD, DSWA_SUPER // DSWA_PERIOD, qw), BF16),
        ],
        compiler_params=_params("parallel"),
        name="proj_even",
    )(x, jnp.asarray(perm, BF16), w)


def _proj_gates_body(x_ref, w_ref, wg_ref, wgt_ref, o_ref, gc_ref, gr_ref, *, tn):
    xb = x_ref[...].astype(BF16)
    for c in range(w_ref.shape[1] // tn):
        o_ref[:, c * tn:(c + 1) * tn] = _dot(xb, w_ref[:, c * tn:(c + 1) * tn]).astype(BF16)
    gc_ref[...] = _dot(xb, wg_ref[...])
    gr_ref[...] = _dot_nt(wgt_ref[...], xb)


def _proj_gates(x, w, wg, wgt):
    n, d = x.shape
    p = w.shape[1]
    tm = TOKEN_TILE
    const = dict(pipeline_mode=pl.Buffered(1))
    return pl.pallas_call(
        functools.partial(_proj_gates_body, tn=512),
        grid=(n // tm,),
        in_specs=[
            pl.BlockSpec((tm, d), lambda i: (i, 0)),
            pl.BlockSpec((d, p), lambda i: (0, 0), **const),
            pl.BlockSpec((d, LANES), lambda i: (0, 0), **const),
            pl.BlockSpec((GATE_COLS, d), lambda i: (0, 0), **const),
        ],
        out_specs=[
            pl.BlockSpec((tm, p), lambda i: (i, 0)),
            pl.BlockSpec((tm, LANES), lambda i: (i, 0)),
            pl.BlockSpec((GATE_COLS, tm), lambda i: (0, i)),
        ],
        out_shape=[
            jax.ShapeDtypeStruct((n, p), BF16),
            jax.ShapeDtypeStruct((n, LANES), F32),
            jax.ShapeDtypeStruct((GATE_COLS, n), F32),
        ],
        compiler_params=_params("parallel"),
        name="proj_odd",
    )(x, w, wg, wgt)


def _dilated_biases():
    def make(q_pos, k_pos, k_is_prev):
        rel = q_pos[:, None] - k_pos[None, :]
        ok = (rel >= 0) & (rel <= DSWA_SPAN)
        variants = [ok, ok & ~k_is_prev[None, :]]
        return jnp.asarray(np.stack([np.where(v, 0.0, NEG_BIG) for v in variants]), F32)

    per = DSWA_PERIOD
    kj = np.arange(2 * BLK)
    b16 = make(np.arange(BLK), kj - BLK, kj < BLK)
    qa, qi = np.divmod(np.arange(BLK), BLK // 4)
    ka, ki = np.divmod(np.arange(2 * BLK), 2 * BLK // 4)
    b4 = make(4 * qi + qa, 4 * (ki - BLK // 4) + ka, ki < BLK // 4)
    qg, qi = np.divmod(np.arange(2 * BLK), 2 * BLK // per)
    kg, ki = np.divmod(np.arange(4 * BLK), 4 * BLK // per)
    b1 = make(per * qi + qg, per * (ki - 2 * BLK // per) + kg, ki < 2 * BLK // per)
    return b16, b4, b1


def _dilated_body(b16_ref, b4_ref, b1_ref, q_ref, kp_ref, kc_ref, vp_ref, vc_ref, y_ref,
                  kf, vf, acc, m_s, l_s, nat):
    per = DSWA_PERIOD
    first = jnp.where(pl.program_id(1) == 0, 1, 0)
    kf[:, :BLK, :] = kp_ref[0]
    kf[:, BLK:, :] = kc_ref[0]
    vf[:, :BLK, :] = vp_ref[0]
    vf[:, BLK:, :] = vc_ref[0]
    scale = LOG2E / math.sqrt(DSWA_HEAD_DIM)

    def low_mask(rows):
        return lax.broadcasted_iota(jnp.int32, (rows, LANES), 1) < DSWA_HEAD_DIM

    def attend(units):
        scores = []
        for q2, kk, _, _ in units:
            low = low_mask(q2.shape[0])
            zero = jnp.zeros_like(q2)
            stacked = jnp.concatenate([jnp.where(low, q2, zero), jnp.where(low, zero, q2)], axis=0)
            scores.append(_dot_nt(stacked, kk))
        soft = []
        for s, (_, _, _, bias) in zip(scores, units):
            s = s + jnp.concatenate([bias, bias], axis=0)
            m = jnp.max(s, axis=-1, keepdims=True)
            p = jnp.exp2(s - m)
            soft.append((m, jnp.sum(p, axis=-1, keepdims=True), p.astype(BF16)))
        return [(m, l, _dot(p, vv)) for (m, l, p), (_, _, vv, _) in zip(soft, units)]

    def by_lane(x):
        r = x.shape[0] // 2
        return jnp.where(low_mask(r), x[:r], x[r:])

    def merge(part, m_old, l_old, acc_old):
        m, l, pv = (by_lane(x) for x in part)
        m_tot = jnp.maximum(m_old, m)
        a_old = jnp.exp2(m_old - m_tot)
        a_new = jnp.exp2(m - m_tot)
        return m_tot, l_old * a_old + l * a_new, acc_old * a_old + pv * a_new

    def gather(ref, lead, groups, rows):
        return jnp.concatenate([ref[lead + (g, rows, slice(None))] for g in groups], axis=0)

    def scatter(ref, groups, rows, value):
        n = value.shape[0] // len(groups)
        for j, g in enumerate(groups):
            ref[g, rows, :] = value[j * n:(j + 1) * n]

    state = (m_s, l_s, acc)

    def fold(parts, where):
        return [merge(part, *(gather(ref, (), groups, rows) for ref in state))
                for part, (groups, rows) in zip(parts, where)]

    def d16(it, carry):
        groups = [4 * it + j for j in range(4)]
        parts = attend([(_scaled_q(q_ref[0, g], scale), kf[g], vf[g], b16_ref[first]) for g in groups])
        for g, part in zip(groups, parts):
            for ref, x in zip(state, part):
                ref[g] = by_lane(x)
        return carry

    lax.fori_loop(0, per // 4, d16, 0)

    def d4(r4, carry):
        nq = BLK // 4
        groups = [4 * a + r4 for a in range(4)]
        units, where = [], []
        for blk in range(4):
            qrows = slice(blk * nq, (blk + 1) * nq)
            krows = slice(BLK - nq + blk * nq, BLK + (blk + 1) * nq)
            bias = b4_ref[first] if blk == 0 else b4_ref[0]
            units.append((_scaled_q(gather(q_ref, (0,), groups, qrows), scale),
                          gather(kf, (), groups, krows), gather(vf, (), groups, krows), bias))
            where.append((groups, qrows))
        for merged, (groups, rows) in zip(fold(attend(units), where), where):
            for ref, x in zip(state, merged):
                scatter(ref, groups, rows, x)
        return carry

    lax.fori_loop(0, 4, d4, 0)

    def d1(it, carry):
        nq = 2 * BLK // per
        groups = list(range(per))
        units, where, blks = [], [], []
        for j in range(2):
            blk = 2 * it + j
            qrows = pl.ds(pl.multiple_of(blk * nq, nq), nq)
            krows = pl.ds(pl.multiple_of(BLK - nq + blk * nq, nq), 2 * nq)
            flag = jnp.where(blk == 0, first, 0)
            units.append((_scaled_q(gather(q_ref, (0,), groups, qrows), scale),
                          gather(kf, (), groups, krows), gather(vf, (), groups, krows), b1_ref[flag]))
            where.append((groups, qrows))
            blks.append(blk)
        for blk, (_, l, a) in zip(blks, fold(attend(units), where)):
            r = a.shape[0]
            out = a / l
            for g in groups:
                nat[pl.ds(g, nq, stride=per), :] = out[g * nq:(g + 1) * nq]
            y_ref[0, pl.ds(pl.multiple_of(blk * r, r), r), :] = nat[...].astype(BF16)
        return carry

    lax.fori_loop(0, DSWA_SUPER // (4 * BLK), d1, 0)


def _dilated_attention(pq, b, s):
    per, w = DSWA_PERIOD, DSWA_W
    pairs = w // LANES
    nsb = s // DSWA_SUPER
    rows = DSWA_SUPER // per
    biases = _dilated_biases()
    blk = (1, per, rows, LANES)
    cur = lambda col: (lambda bi, sb, hp: (bi * nsb + sb, 0, 0, col * pairs + hp))
    prev = lambda col: (lambda bi, sb, hp: (bi * nsb + jnp.maximum(sb - 1, 0), 0, 0, col * pairs + hp))
    const = lambda a: pl.BlockSpec(a.shape, lambda bi, sb, hp: (0, 0, 0))
    return pl.pallas_call(
        _dilated_body,
        grid=(b, nsb, pairs),
        in_specs=[const(a) for a in biases] + [
            pl.BlockSpec(blk, cur(0)),
            pl.BlockSpec(blk, prev(1)),
            pl.BlockSpec(blk, cur(1)),
            pl.BlockSpec(blk, prev(2)),
            pl.BlockSpec(blk, cur(2)),
        ],
        out_specs=pl.BlockSpec((1, DSWA_SUPER, LANES), lambda bi, sb, hp: (bi * nsb + sb, 0, hp)),
        out_shape=jax.ShapeDtypeStruct((b * nsb, DSWA_SUPER, w), BF16),
        scratch_shapes=[
            pltpu.VMEM((per, 2 * rows, LANES), BF16),
            pltpu.VMEM((per, 2 * rows, LANES), BF16),
            pltpu.VMEM((per, rows, LANES), F32),
            pltpu.VMEM((per, rows, LANES), F32),
            pltpu.VMEM((per, rows, LANES), F32),
            pltpu.VMEM((2 * BLK, LANES), F32),
        ],
        compiler_params=_params("parallel", "parallel", "parallel"),
        name="dilated",
    )(*biases, pq, pq, pq, pq, pq).reshape(b * s, w)


def _outproj_even_body(p_ref, halo_ref, y_ref, cw_ref, wo_ref, x_ref, g_ref, b_ref, o_ref, *, seq, tm, halo):
    c = CONV_CH
    pc = p_ref[...].astype(F32)
    bg, u = pc[:, :c], pc[:, c:2 * c] * pc[:, 2 * c:3 * c]
    ph = halo_ref[...].astype(F32)
    uh = ph[:, c:2 * c] * ph[:, 2 * c:3 * c]
    starts_sequence = (pl.program_id(0) * tm) % seq == 0
    uh = jnp.where(starts_sequence, 0.0, uh)
    full = jnp.concatenate([uh, u], axis=0)
    cw = cw_ref[...]
    conv = cw[2:3, :] * u + cw[1:2, :] * full[halo - 1:halo - 1 + tm] + cw[0:1, :] * full[halo - 2:halo - 2 + tm]
    y_conv = (bg * conv).astype(BF16)
    mix = _dot(y_conv, wo_ref[:c, :]) + _dot(y_ref[...], wo_ref[c:, :])
    z = ALPHA * x_ref[...] + mix
    o_ref[...] = _layer_norm(z, g_ref[...], b_ref[...])


def _outproj_even(p, y_att, conv_w, w_out, x, g, b, seq):
    n, d = x.shape
    tm = TOKEN_TILE
    halo = 16
    cw = 3 * CONV_CH
    return pl.pallas_call(
        functools.partial(_outproj_even_body, seq=seq, tm=tm, halo=halo),
        grid=(n // tm,),
        in_specs=[
            pl.BlockSpec((tm, cw), lambda i: (i, 0)),
            pl.BlockSpec((halo, cw), lambda i: (jnp.maximum(i * (tm // halo) - 1, 0), 0)),
            pl.BlockSpec((tm, DSWA_W), lambda i: (i, 0)),
            pl.BlockSpec((CONV_WIDTH, CONV_CH), lambda i: (0, 0)),
            pl.BlockSpec((d, d), lambda i: (0, 0), pipeline_mode=pl.Buffered(1)),
            pl.BlockSpec((tm, d), lambda i: (i, 0)),
            pl.BlockSpec((1, d), lambda i: (0, 0)),
            pl.BlockSpec((1, d), lambda i: (0, 0)),
        ],
        out_specs=pl.BlockSpec((tm, d), lambda i: (i, 0)),
        out_shape=jax.ShapeDtypeStruct((n, d), F32),
        compiler_params=_params("parallel"),
        name="outproj_even",
    )(p, p, y_att, conv_w, w_out, x, g.reshape(1, d), b.reshape(1, d))


def _mlstm_body(p_ref, gc_ref, gr_ref, bc_ref, br_ref, ng_ref, y_ref, c_scr, n_scr, m_scr):
    L, dh, nh = MLSTM_CHUNK, MLSTM_HEAD_DIM, MLSTM_HEADS

    @pl.when(pl.program_id(1) == 0)
    def _():
        c_scr[...] = jnp.zeros_like(c_scr)
        n_scr[...] = jnp.zeros_like(n_scr)
        m_scr[...] = jnp.zeros_like(m_scr)

    row = lax.broadcasted_iota(jnp.int32, (L, L), 0)
    col = lax.broadcasted_iota(jnp.int32, (L, L), 1)
    causal = col <= row
    tri = causal.astype(BF16)
    tri_t = (row <= col).astype(BF16)

    gcol = gc_ref[...] + bc_ref[...]
    grow = gr_ref[...] + br_ref[...]
    bcum_c = _split_dot(tri, _neg_softplus(-gcol))
    bcum_r = _split_dot_r(_neg_softplus(-grow), tri_t)
    scale = 1.0 / math.sqrt(dh)

    heads = range(nh)
    part = lambda j, h: p_ref[:, j * MLSTM_W + h * dh:j * MLSTM_W + (h + 1) * dh]
    q, k, v = ([part(j, h) for h in heads] for j in range(3))
    rep = lambda a: jnp.broadcast_to(a, (L, LANES))
    i_c = [rep(gcol[:, h:h + 1]) for h in heads]
    b_c = [rep(bcum_c[:, nh + h:nh + h + 1]) for h in heads]
    i_r = [grow[h:h + 1, :] for h in heads]
    b_r = [bcum_r[nh + h:nh + h + 1, :] for h in heads]
    m_prev = [m_scr[h] for h in heads]
    c_prev = [c_scr[h] for h in heads]
    n_prev = [n_scr[h] for h in heads]

    s = [_dot_nt(q[h], k[h]) for h in heads]
    q_c = [_dot(q[h], c_prev[h].astype(BF16)) for h in heads]

    for h in heads:
        b_tot = b_c[h][L - 1:L, :]
        d_state = b_tot - b_c[h] + i_c[h]
        m_new = jnp.maximum(b_tot + m_prev[h], jnp.max(d_state, axis=0, keepdims=True))
        kw = k[h].astype(F32) * (jnp.exp(d_state - m_new) * scale)
        decay = jnp.exp(b_tot + m_prev[h] - m_new)
        c_scr[h] = decay * c_prev[h] + _dot_tn(kw.astype(BF16), v[h])
        n_scr[h] = decay * n_prev[h] + jnp.sum(kw, axis=0, keepdims=True)
        m_scr[h] = m_new

    w_inter, m_t, qk = [], [], []
    for h in heads:
        d_intra = jnp.where(causal, b_c[h] - b_r[h] + i_r[h], NEG_BIG)
        d_inter = b_c[h] + m_prev[h]
        m_t.append(jnp.maximum(d_inter, jnp.max(d_intra, axis=-1, keepdims=True)))
        w_inter.append(jnp.exp(d_inter - m_t[h]))
        qk.append(s[h] * scale * jnp.exp(d_intra - m_t[h]))
    qk_v = [_dot(qk[h].astype(BF16), v[h]) for h in heads]

    for h in heads:
        cols = slice(h * dh, (h + 1) * dh)
        num = w_inter[h] * q_c[h] + qk_v[h]
        den = (w_inter[h] * jnp.sum(q[h].astype(F32) * n_prev[h], axis=-1, keepdims=True)
               + jnp.sum(qk[h], axis=-1, keepdims=True))
        h_tilde = num / jnp.maximum(jnp.abs(den), jnp.exp(-m_t[h]))
        cell = jax.nn.sigmoid(part(3, h).astype(F32)) * h_tilde
        mu = jnp.mean(cell, axis=-1, keepdims=True)
        cc = cell - mu
        var = jnp.mean(cc * cc, axis=-1, keepdims=True)
        y_ref[:, cols] = (cc * lax.rsqrt(var + LN_EPS) * ng_ref[:, cols]).astype(BF16)


def _mlstm(p, gcol, grow, b_i, b_f, norm_g, b, s):
    L = MLSTM_CHUNK
    nc = s // L
    bias = jnp.concatenate([b_i, b_f]).astype(F32)
    bias_c = jnp.zeros((1, LANES), F32).at[0, :GATE_COLS].set(bias)
    bias_r = bias.reshape(GATE_COLS, 1)
    tok = lambda bi, c: (bi * nc + c, 0)
    return pl.pallas_call(
        _mlstm_body,
        grid=(b, nc),
        in_specs=[
            pl.BlockSpec((L, 4 * MLSTM_W), tok),
            pl.BlockSpec((L, LANES), tok),
            pl.BlockSpec((GATE_COLS, L), lambda bi, c: (0, bi * nc + c)),
            pl.BlockSpec((1, LANES), lambda bi, c: (0, 0)),
            pl.BlockSpec((GATE_COLS, 1), lambda bi, c: (0, 0)),
            pl.BlockSpec((1, MLSTM_W), lambda bi, c: (0, 0)),
        ],
        out_specs=pl.BlockSpec((L, MLSTM_W), tok),
        out_shape=jax.ShapeDtypeStruct((b * s, MLSTM_W), BF16),
        scratch_shapes=[
            pltpu.VMEM((MLSTM_HEADS, MLSTM_HEAD_DIM, MLSTM_HEAD_DIM), F32),
            pltpu.VMEM((MLSTM_HEADS, 1, MLSTM_HEAD_DIM), F32),
            pltpu.VMEM((MLSTM_HEADS, 1, LANES), F32),
        ],
        compiler_params=_params("parallel", "arbitrary"),
        name="mlstm",
    )(p, gcol, grow, bias_c, bias_r, norm_g.reshape(1, MLSTM_W))


def _stick_body(bias_ref, q_ref, k_ref, v_ref, y_ref, acc_scr, c_scr):
    t = SB_TILE
    units = range(SB_UNITS)
    qbs = [pl.program_id(2) * SB_UNITS + u for u in units]
    row = lax.broadcasted_iota(jnp.int32, (t, t), 0)
    col = lax.broadcasted_iota(jnp.int32, (t, t), 1)
    suffix = (row >= col).astype(BF16)
    low = lax.broadcasted_iota(jnp.int32, (t, LANES), 1) < SB_HEAD_DIM

    def stacked(q2):
        zero = jnp.zeros_like(q2)
        return jnp.concatenate([jnp.where(low, q2, zero), jnp.where(low, zero, q2)], axis=0)

    def keys(j):
        return pl.ds(pl.multiple_of(j * t, t), t)

    def weights(z, incl, c_later):
        return jnp.exp2(z + incl + c_later).astype(BF16)

    def log_mass(lg):
        return jnp.sum(lg, axis=-1, keepdims=True)

    qs = [stacked(_scaled_q(q_ref[0, u * t:(u + 1) * t, :], LOG2E / math.sqrt(SB_HEAD_DIM))) for u in units]
    j_prev = [jnp.maximum(qb - 1, 0) for qb in qbs]
    causal = bias_ref[...]
    causal = jnp.concatenate([causal, causal], axis=0)
    z_diag = [_dot_nt(qs[u], k_ref[0, keys(qbs[u]), :]) + causal for u in units]
    z_prev = [_dot_nt(qs[u], k_ref[0, keys(j_prev[u]), :]) + jnp.where(qbs[u] > 0, 0.0, NEG_BIG) for u in units]
    lg_diag = [_neg_softplus2(z) for z in z_diag]
    lg_prev = [_neg_softplus2(z) for z in z_prev]
    in_diag = [_split_dot_r(lg, suffix) for lg in lg_diag]
    in_prev = [_split_dot_r(lg, suffix) for lg in lg_prev]
    c_diag = [log_mass(lg) for lg in lg_diag]
    a_diag = [weights(z_diag[u], in_diag[u], 0.0) for u in units]
    a_prev = [weights(z_prev[u], in_prev[u], c_diag[u]) for u in units]
    c_both = [c_diag[u] + log_mass(lg_prev[u]) for u in units]
    for u in units:
        acc_scr[u] = (_dot(a_diag[u], v_ref[0, keys(qbs[u]), :])
                      + _dot(a_prev[u], v_ref[0, keys(j_prev[u]), :]))
        c_scr[u] = jnp.broadcast_to(c_both[u], c_scr.shape[1:])

    def cond(carry):
        j, c_max = carry
        return (j >= 0) & (c_max > SB_LOG2_FLOOR)

    for u in units:
        def body(carry, u=u):
            j, _ = carry
            c_old = c_scr[u]
            z = _dot_nt(qs[u], k_ref[0, keys(j), :])
            lg = _neg_softplus2(z)
            a = weights(z, _split_dot_r(lg, suffix), jnp.concatenate([c_old] * (t // LANES), axis=1))
            acc_scr[u] += _dot(a, v_ref[0, keys(j), :])
            c_new = c_old + log_mass(lg)
            c_scr[u] = c_new
            return j - 1, jnp.max(c_new)

        lax.while_loop(cond, body, (qbs[u] - 2, jnp.max(c_both[u])))
        out = acc_scr[u]
        y_ref[0, u * t:(u + 1) * t, :] = jnp.where(low, out[:t], out[t:]).astype(BF16)


def _stick_breaking(p, b, s, col0):
    t = SB_TILE
    step = SB_UNITS * t
    pairs = SB_W // LANES
    pv = p.reshape(b, s, p.shape[1])
    c0 = col0 // LANES
    idx = np.arange(t)
    causal = jnp.asarray(np.where(idx[None, :] < idx[:, None], 0.0, NEG_BIG), F32)
    return pl.pallas_call(
        _stick_body,
        grid=(b, pairs, s // step),
        in_specs=[
            pl.BlockSpec((t, t), lambda bi, hp, qi: (0, 0)),
            pl.BlockSpec((1, step, LANES), lambda bi, hp, qi: (bi, qi, c0 + hp)),
            pl.BlockSpec((1, s, LANES), lambda bi, hp, qi: (bi, 0, c0 + pairs + hp)),
            pl.BlockSpec((1, s, LANES), lambda bi, hp, qi: (bi, 0, c0 + 2 * pairs + hp)),
        ],
        out_specs=pl.BlockSpec((1, step, LANES), lambda bi, hp, qi: (bi, qi, hp)),
        out_shape=jax.ShapeDtypeStruct((b, s, SB_W), BF16),
        scratch_shapes=[pltpu.VMEM((SB_UNITS, 2 * t, LANES), F32), pltpu.VMEM((SB_UNITS, 2 * t, LANES), F32)],
        compiler_params=_params("parallel", "parallel", "arbitrary"),
        name="stick_breaking",
    )(causal, pv, pv, pv).reshape(b * s, SB_W)


def _outproj_odd_body(ym_ref, ys_ref, wo_ref, x_ref, g_ref, b_ref, o_ref):
    mix = _dot(ym_ref[...], wo_ref[:MLSTM_W, :]) + _dot(ys_ref[...], wo_ref[MLSTM_W:, :])
    z = ALPHA * x_ref[...] + mix
    o_ref[...] = _layer_norm(z, g_ref[...], b_ref[...])


def _outproj_odd(y_m, y_s, w_out, x, g, b):
    n, d = x.shape
    tm = TOKEN_TILE
    return pl.pallas_call(
        _outproj_odd_body,
        grid=(n // tm,),
        in_specs=[
            pl.BlockSpec((tm, MLSTM_W), lambda i: (i, 0)),
            pl.BlockSpec((tm, SB_W), lambda i: (i, 0)),
            pl.BlockSpec((d, d), lambda i: (0, 0), pipeline_mode=pl.Buffered(1)),
            pl.BlockSpec((tm, d), lambda i: (i, 0)),
            pl.BlockSpec((1, d), lambda i: (0, 0)),
            pl.BlockSpec((1, d), lambda i: (0, 0)),
        ],
        out_specs=pl.BlockSpec((tm, d), lambda i: (i, 0)),
        out_shape=jax.ShapeDtypeStruct((n, d), F32),
        compiler_params=_params("parallel"),
        name="outproj_odd",
    )(y_m, y_s, w_out, x, g.reshape(1, d), b.reshape(1, d))


def _even_mixer(x, w_in, conv_w, w_out, g, b, batch, seq):
    pc, pq = _proj_even(x, w_in.astype(BF16))
    y_att = _dilated_attention(pq, batch, seq)
    return _outproj_even(pc, y_att, conv_w, w_out.astype(BF16), x, g, b, seq)


def _odd_mixer(x, w_in, b_i, b_f, norm_g, w_out, g, b, batch, seq):
    gate0 = 4 * MLSTM_W
    w_main = jnp.concatenate([w_in[:, :gate0], w_in[:, gate0 + GATE_COLS:]], axis=1).astype(BF16)
    w_gate = w_in[:, gate0:gate0 + GATE_COLS].astype(BF16)
    w_gate_pad = jnp.pad(w_gate, ((0, 0), (0, LANES - GATE_COLS)))
    p, gcol, grow = _proj_gates(x, w_main, w_gate_pad, w_gate.T)
    y_m = _mlstm(p, gcol, grow, b_i, b_f, norm_g, batch, seq)
    y_s = _stick_breaking(p, batch, seq, gate0)
    return _outproj_odd(y_m, y_s, w_out.astype(BF16), x, g, b)


def kernel(x, l0_ffn1_w_in, l0_ffn1_w_out, l0_ln1_g, l0_ln1_b, l0_mix_w_in, l0_conv_w, l0_mix_w_out, l0_ln2_g, l0_ln2_b, l0_ffn2_w_in, l0_ffn2_w_out, l0_ln3_g, l0_ln3_b, l1_ffn1_w_in, l1_ffn1_w_out, l1_ln1_g, l1_ln1_b, l1_mix_w_in, l1_mlstm_b_i, l1_mlstm_b_f, l1_mlstm_norm_g, l1_mix_w_out, l1_ln2_g, l1_ln2_b, l1_ffn2_w_in, l1_ffn2_w_out, l1_ln3_g, l1_ln3_b):
    batch, seq, d = x.shape
    ffn = lambda t, w_in, w_out, g, b: _ffn_ln(t, w_in.astype(BF16), w_out.astype(BF16), g, b)
    t = x.reshape(batch * seq, d)
    t = ffn(t, l0_ffn1_w_in, l0_ffn1_w_out, l0_ln1_g, l0_ln1_b)
    t = _even_mixer(t, l0_mix_w_in, l0_conv_w, l0_mix_w_out, l0_ln2_g, l0_ln2_b, batch, seq)
    t = ffn(t, l0_ffn2_w_in, l0_ffn2_w_out, l0_ln3_g, l0_ln3_b)
    t = ffn(t, l1_ffn1_w_in, l1_ffn1_w_out, l1_ln1_g, l1_ln1_b)
    t = _odd_mixer(t, l1_mix_w_in, l1_mlstm_b_i, l1_mlstm_b_f, l1_mlstm_norm_g, l1_mix_w_out,
                   l1_ln2_g, l1_ln2_b, batch, seq)
    t = ffn(t, l1_ffn2_w_in, l1_ffn2_w_out, l1_ln3_g, l1_ln3_b)
    return t.reshape(batch, seq, d)
```

```python
import functools
import math

import jax
import jax.numpy as jnp
import numpy as np
from jax import lax
from jax.experimental import pallas as pl
from jax.experimental.pallas import tpu as pltpu

F32 = jnp.float32
BF16 = jnp.bfloat16

DEPTH = 2
ALPHA = (2 * DEPTH) ** 0.25
FFN_RES_WEIGHT = 0.5
LN_EPS = 1e-5

LANES = 128
MXU_TILE = 256
BLK = 128
CONV_CH = 256
CONV_WIDTH = 3
DSWA_HEAD_DIM = 64
DSWA_W = 768
DSWA_SPAN = 128
DSWA_PERIOD = 16
DSWA_SUPER = DSWA_PERIOD * BLK
MLSTM_HEADS = 4
MLSTM_W = 512
MLSTM_HEAD_DIM = 128
MLSTM_CHUNK = 128
SB_HEAD_DIM = 64
SB_W = 512
GATE_COLS = 2 * MLSTM_HEADS

TOKEN_TILE = 512
SB_TILE = 256
SB_UNITS = 2
SB_LOG2_FLOOR = -127.0
LOG2E = math.log2(math.e)
NEG_BIG = -1e30
VMEM_LIMIT = 56 * 1024 * 1024


def _params(*sem):
    return pltpu.CompilerParams(dimension_semantics=sem, vmem_limit_bytes=VMEM_LIMIT)


def _layer_norm(z, g, b):
    mu = jnp.mean(z, axis=-1, keepdims=True)
    zc = z - mu
    var = jnp.mean(zc * zc, axis=-1, keepdims=True)
    return zc * lax.rsqrt(var + LN_EPS) * g + b


def _dot(a, b):
    return jnp.dot(a, b, preferred_element_type=F32)


def _dot_nt(a, b):
    return lax.dot_general(a, b, (((1,), (1,)), ((), ())), preferred_element_type=F32)


def _dot_tn(a, b):
    return lax.dot_general(a, b, (((0,), (0,)), ((), ())), preferred_element_type=F32)


def _split_dot(tri, x):
    hi = x.astype(BF16)
    lo = (x - hi.astype(F32)).astype(BF16)
    return _dot(tri, hi) + _dot(tri, lo)


def _split_dot_r(x, tri):
    hi = x.astype(BF16)
    lo = (x - hi.astype(F32)).astype(BF16)
    return _dot(hi, tri) + _dot(lo, tri)


def _neg_softplus(z):
    return -(jnp.maximum(z, 0.0) + jnp.log(1.0 + jnp.exp(-jnp.abs(z))))


def _neg_softplus2(z2):
    return -(jnp.maximum(z2, 0.0) + jnp.log(1.0 + jnp.exp2(-jnp.abs(z2))) * LOG2E)


def _scaled_q(q, scale):
    return (q.astype(F32) * scale).astype(BF16)


def _ffn_chunks(d_ff):
    tiles = d_ff // MXU_TILE
    cut = (tiles + 1) // 2 * MXU_TILE
    return ((0, cut), (cut, d_ff))


def _ffn_block(x, win_ref, wout_ref, g, b):
    d_ff = wout_ref.shape[0]
    xb = x.astype(BF16)
    acc = None
    for lo, hi in _ffn_chunks(d_ff):
        gate = _dot(xb, win_ref[:, lo:hi])
        up = _dot(xb, win_ref[:, d_ff + lo:d_ff + hi])
        act = (gate * jax.nn.sigmoid(gate) * up).astype(BF16)
        part = _dot(act, wout_ref[lo:hi, :])
        acc = part if acc is None else acc + part
    return _layer_norm(ALPHA * x + FFN_RES_WEIGHT * acc, g, b)


def _ffn_specs(d, d_ff):
    const = dict(pipeline_mode=pl.Buffered(1))
    return [
        pl.BlockSpec((d, 2 * d_ff), lambda i: (0, 0), **const),
        pl.BlockSpec((d_ff, d), lambda i: (0, 0), **const),
        pl.BlockSpec((1, d), lambda i: (0, 0)),
        pl.BlockSpec((1, d), lambda i: (0, 0)),
    ]


def _ffn_ln_body(x_ref, win_ref, wout_ref, g_ref, b_ref, o_ref):
    o_ref[...] = _ffn_block(x_ref[...], win_ref, wout_ref, g_ref[...], b_ref[...])


def _ffn_ln(x, ffn):
    n, d = x.shape
    w_in, w_out, g, b = ffn
    d_ff = w_out.shape[0]
    tm = TOKEN_TILE
    return pl.pallas_call(
        _ffn_ln_body,
        grid=(n // tm,),
        in_specs=[pl.BlockSpec((tm, d), lambda i: (i, 0))] + _ffn_specs(d, d_ff),
        out_specs=pl.BlockSpec((tm, d), lambda i: (i, 0)),
        out_shape=jax.ShapeDtypeStruct((n, d), F32),
        compiler_params=_params("parallel"),
        name="ffn_ln",
    )(x, w_in, w_out, g.reshape(1, d), b.reshape(1, d))


def _proj_even_body(x_ref, perm_ref, w_ref, pc_ref, pq_ref, *, tm, tn):
    cw = 3 * CONV_CH
    xb = x_ref[...].astype(BF16)
    pc_ref[...] = _dot(xb, w_ref[:, :cw]).astype(BF16)
    xg = _dot(perm_ref[...], xb).astype(BF16)
    rows = tm // DSWA_PERIOD
    for c in range((w_ref.shape[1] - cw) // tn):
        res = _dot(xg, w_ref[:, cw + c * tn:cw + (c + 1) * tn]).astype(BF16)
        pq_ref[0, :, :, c * tn:(c + 1) * tn] = res.reshape(DSWA_PERIOD, rows, tn)


def _proj_even(x, w):
    n, d = x.shape
    cw = 3 * CONV_CH
    qw = w.shape[1] - cw
    tm = TOKEN_TILE
    tiles = DSWA_SUPER // tm
    rows = tm // DSWA_PERIOD
    dst = np.arange(tm)
    perm = np.zeros((tm, tm), np.float32)
    perm[dst, (dst % rows) * DSWA_PERIOD + dst // rows] = 1.0
    const = dict(pipeline_mode=pl.Buffered(1))
    return pl.pallas_call(
        functools.partial(_proj_even_body, tm=tm, tn=768),
        grid=(n // tm,),
        in_specs=[
            pl.BlockSpec((tm, d), lambda i: (i, 0)),
            pl.BlockSpec((tm, tm), lambda i: (0, 0), **const),
            pl.BlockSpec((d, cw + qw), lambda i: (0, 0), **const),
        ],
        out_specs=[
            pl.BlockSpec((tm, cw), lambda i: (i, 0)),
            pl.BlockSpec((1, DSWA_PERIOD, rows, qw), lambda i: (i // tiles, 0, i % tiles, 0)),
        ],
        out_shape=[
            jax.ShapeDtypeStruct((n, cw), BF16),
            jax.ShapeDtypeStruct((n // DSWA_SUPER, DSWA_PERIOD, DSWA_SUPER // DSWA_PERIOD, qw), BF16),
        ],
        compiler_params=_params("parallel"),
        name="proj_even",
    )(x, jnp.asarray(perm, BF16), w)


def _proj_gates_body(x_ref, w_ref, wg_ref, wgt_ref, o_ref, gc_ref, gr_ref, *, tn):
    xb = x_ref[...].astype(BF16)
    for c in range(w_ref.shape[1] // tn):
        o_ref[:, c * tn:(c + 1) * tn] = _dot(xb, w_ref[:, c * tn:(c + 1) * tn]).astype(BF16)
    gc_ref[...] = _dot(xb, wg_ref[...])
    gr_ref[...] = _dot_nt(wgt_ref[...], xb)


def _proj_gates(x, w, wg, wgt):
    n, d = x.shape
    p = w.shape[1]
    tm = TOKEN_TILE
    const = dict(pipeline_mode=pl.Buffered(1))
    return pl.pallas_call(
        functools.partial(_proj_gates_body, tn=512),
        grid=(n // tm,),
        in_specs=[
            pl.BlockSpec((tm, d), lambda i: (i, 0)),
            pl.BlockSpec((d, p), lambda i: (0, 0), **const),
            pl.BlockSpec((d, LANES), lambda i: (0, 0), **const),
            pl.BlockSpec((GATE_COLS, d), lambda i: (0, 0), **const),
        ],
        out_specs=[
            pl.BlockSpec((tm, p), lambda i: (i, 0)),
            pl.BlockSpec((tm, LANES), lambda i: (i, 0)),
            pl.BlockSpec((GATE_COLS, tm), lambda i: (0, i)),
        ],
        out_shape=[
            jax.ShapeDtypeStruct((n, p), BF16),
            jax.ShapeDtypeStruct((n, LANES), F32),
            jax.ShapeDtypeStruct((GATE_COLS, n), F32),
        ],
        compiler_params=_params("parallel"),
        name="proj_odd",
    )(x, w, wg, wgt)


def _dilated_biases():
    def make(q_pos, k_pos, k_is_prev):
        rel = q_pos[:, None] - k_pos[None, :]
        ok = (rel >= 0) & (rel <= DSWA_SPAN)
        variants = [ok, ok & ~k_is_prev[None, :]]
        return jnp.asarray(np.stack([np.where(v, 0.0, NEG_BIG) for v in variants]), F32)

    per = DSWA_PERIOD
    kj = np.arange(2 * BLK)
    b16 = make(np.arange(BLK), kj - BLK, kj < BLK)
    qa, qi = np.divmod(np.arange(BLK), BLK // 4)
    ka, ki = np.divmod(np.arange(2 * BLK), 2 * BLK // 4)
    b4 = make(4 * qi + qa, 4 * (ki - BLK // 4) + ka, ki < BLK // 4)
    qg, qi = np.divmod(np.arange(2 * BLK), 2 * BLK // per)
    kg, ki = np.divmod(np.arange(4 * BLK), 4 * BLK // per)
    b1 = make(per * qi + qg, per * (ki - 2 * BLK // per) + kg, ki < 2 * BLK // per)
    return b16, b4, b1


def _dilated_body(b16_ref, b4_ref, b1_ref, q_ref, kp_ref, kc_ref, vp_ref, vc_ref, y_ref,
                  kf, vf, acc, m_s, l_s, nat):
    per = DSWA_PERIOD
    first = jnp.where(pl.program_id(1) == 0, 1, 0)
    kf[:, :BLK, :] = kp_ref[0]
    kf[:, BLK:, :] = kc_ref[0]
    vf[:, :BLK, :] = vp_ref[0]
    vf[:, BLK:, :] = vc_ref[0]
    scale = LOG2E / math.sqrt(DSWA_HEAD_DIM)

    def low_mask(rows):
        return lax.broadcasted_iota(jnp.int32, (rows, LANES), 1) < DSWA_HEAD_DIM

    def attend(units):
        scores = []
        for q2, kk, _, _ in units:
            low = low_mask(q2.shape[0])
            zero = jnp.zeros_like(q2)
            stacked = jnp.concatenate([jnp.where(low, q2, zero), jnp.where(low, zero, q2)], axis=0)
            scores.append(_dot_nt(stacked, kk))
        soft = []
        for s, (_, _, _, bias) in zip(scores, units):
            s = s + jnp.concatenate([bias, bias], axis=0)
            m = jnp.max(s, axis=-1, keepdims=True)
            p = jnp.exp2(s - m)
            soft.append((m, jnp.sum(p, axis=-1, keepdims=True), p.astype(BF16)))
        return [(m, l, _dot(p, vv)) for (m, l, p), (_, _, vv, _) in zip(soft, units)]

    def by_lane(x):
        r = x.shape[0] // 2
        return jnp.where(low_mask(r), x[:r], x[r:])

    def merge(part, m_old, l_old, acc_old):
        m, l, pv = (by_lane(x) for x in part)
        m_tot = jnp.maximum(m_old, m)
        a_old = jnp.exp2(m_old - m_tot)
        a_new = jnp.exp2(m - m_tot)
        return m_tot, l_old * a_old + l * a_new, acc_old * a_old + pv * a_new

    def gather(ref, lead, groups, rows):
        return jnp.concatenate([ref[lead + (g, rows, slice(None))] for g in groups], axis=0)

    def scatter(ref, groups, rows, value):
        n = value.shape[0] // len(groups)
        for j, g in enumerate(groups):
            ref[g, rows, :] = value[j * n:(j + 1) * n]

    state = (m_s, l_s, acc)

    def fold(parts, where):
        return [merge(part, *(gather(ref, (), groups, rows) for ref in state))
                for part, (groups, rows) in zip(parts, where)]

    def d16(it, carry):
        groups = [8 * it + j for j in range(8)]
        parts = attend([(_scaled_q(q_ref[0, g], scale), kf[g], vf[g], b16_ref[first]) for g in groups])
        for g, part in zip(groups, parts):
            for ref, x in zip(state, part):
                ref[g] = by_lane(x)
        return carry

    lax.fori_loop(0, per // 8, d16, 0)

    def d4(r4, carry):
        nq = BLK // 4
        groups = [4 * a + r4 for a in range(4)]
        units, where = [], []
        for blk in range(4):
            qrows = slice(blk * nq, (blk + 1) * nq)
            krows = slice(BLK - nq + blk * nq, BLK + (blk + 1) * nq)
            bias = b4_ref[first] if blk == 0 else b4_ref[0]
            units.append((_scaled_q(gather(q_ref, (0,), groups, qrows), scale),
                          gather(kf, (), groups, krows), gather(vf, (), groups, krows), bias))
            where.append((groups, qrows))
        for merged, (groups, rows) in zip(fold(attend(units), where), where):
            for ref, x in zip(state, merged):
                scatter(ref, groups, rows, x)
        return carry

    lax.fori_loop(0, 4, d4, 0)

    def d1(it, carry):
        nq = 2 * BLK // per
        groups = list(range(per))
        units, where, blks = [], [], []
        for j in range(2):
            blk = 2 * it + j
            qrows = pl.ds(pl.multiple_of(blk * nq, nq), nq)
            krows = pl.ds(pl.multiple_of(BLK - nq + blk * nq, nq), 2 * nq)
            flag = jnp.where(blk == 0, first, 0)
            units.append((_scaled_q(gather(q_ref, (0,), groups, qrows), scale),
                          gather(kf, (), groups, krows), gather(vf, (), groups, krows), b1_ref[flag]))
            where.append((groups, qrows))
            blks.append(blk)
        for j, (blk, (_, l, a)) in enumerate(zip(blks, fold(attend(units), where))):
            r = a.shape[0]
            out = a / l
            for g in groups:
                nat[j, pl.ds(g, nq, stride=per), :] = out[g * nq:(g + 1) * nq]
            y_ref[0, pl.ds(pl.multiple_of(blk * r, r), r), :] = nat[j].astype(BF16)
        return carry

    lax.fori_loop(0, DSWA_SUPER // (4 * BLK), d1, 0)


def _dilated_attention(pq, b, s):
    per, w = DSWA_PERIOD, DSWA_W
    pairs = w // LANES
    nsb = s // DSWA_SUPER
    rows = DSWA_SUPER // per
    biases = _dilated_biases()
    blk = (1, per, rows, LANES)
    cur = lambda col: (lambda bi, sb, hp: (bi * nsb + sb, 0, 0, col * pairs + hp))
    prev = lambda col: (lambda bi, sb, hp: (bi * nsb + jnp.maximum(sb - 1, 0), 0, 0, col * pairs + hp))
    const = lambda a: pl.BlockSpec(a.shape, lambda bi, sb, hp: (0, 0, 0))
    return pl.pallas_call(
        _dilated_body,
        grid=(b, nsb, pairs),
        in_specs=[const(a) for a in biases] + [
            pl.BlockSpec(blk, cur(0)),
            pl.BlockSpec(blk, prev(1)),
            pl.BlockSpec(blk, cur(1)),
            pl.BlockSpec(blk, prev(2)),
            pl.BlockSpec(blk, cur(2)),
        ],
        out_specs=pl.BlockSpec((1, DSWA_SUPER, LANES), lambda bi, sb, hp: (bi * nsb + sb, 0, hp)),
        out_shape=jax.ShapeDtypeStruct((b * nsb, DSWA_SUPER, w), BF16),
        scratch_shapes=[
            pltpu.VMEM((per, 2 * rows, LANES), BF16),
            pltpu.VMEM((per, 2 * rows, LANES), BF16),
            pltpu.VMEM((per, rows, LANES), F32),
            pltpu.VMEM((per, rows, LANES), F32),
            pltpu.VMEM((per, rows, LANES), F32),
            pltpu.VMEM((2, 2 * BLK, LANES), F32),
        ],
        compiler_params=_params("parallel", "parallel", "parallel"),
        name="dilated",
    )(*biases, pq, pq, pq, pq, pq).reshape(b * s, w)


def _outproj_even_body(p_ref, halo_ref, y_ref, cw_ref, wo_ref, x_ref, g_ref, b_ref,
                       win_ref, wout_ref, g3_ref, b3_ref, o_ref, *, seq, tm, halo):
    c = CONV_CH
    pc = p_ref[...].astype(F32)
    bg, u = pc[:, :c], pc[:, c:2 * c] * pc[:, 2 * c:3 * c]
    ph = halo_ref[...].astype(F32)
    uh = ph[:, c:2 * c] * ph[:, 2 * c:3 * c]
    starts_sequence = (pl.program_id(0) * tm) % seq == 0
    uh = jnp.where(starts_sequence, 0.0, uh)
    full = jnp.concatenate([uh, u], axis=0)
    cw = cw_ref[...]
    conv = cw[2:3, :] * u + cw[1:2, :] * full[halo - 1:halo - 1 + tm] + cw[0:1, :] * full[halo - 2:halo - 2 + tm]
    y_conv = (bg * conv).astype(BF16)
    mix = _dot(y_conv, wo_ref[:c, :]) + _dot(y_ref[...], wo_ref[c:, :])
    mixed = _layer_norm(ALPHA * x_ref[...] + mix, g_ref[...], b_ref[...])
    o_ref[...] = _ffn_block(mixed, win_ref, wout_ref, g3_ref[...], b3_ref[...])


def _outproj_even(p, y_att, conv_w, w_out, x, g, b, seq, ffn):
    n, d = x.shape
    ffn_w_in, ffn_w_out, ffn_g, ffn_b = ffn
    tm = TOKEN_TILE
    halo = 16
    cw = 3 * CONV_CH
    return pl.pallas_call(
        functools.partial(_outproj_even_body, seq=seq, tm=tm, halo=halo),
        grid=(n // tm,),
        in_specs=[
            pl.BlockSpec((tm, cw), lambda i: (i, 0)),
            pl.BlockSpec((halo, cw), lambda i: (jnp.maximum(i * (tm // halo) - 1, 0), 0)),
            pl.BlockSpec((tm, DSWA_W), lambda i: (i, 0)),
            pl.BlockSpec((CONV_WIDTH, CONV_CH), lambda i: (0, 0)),
            pl.BlockSpec((d, d), lambda i: (0, 0), pipeline_mode=pl.Buffered(1)),
            pl.BlockSpec((tm, d), lambda i: (i, 0)),
            pl.BlockSpec((1, d), lambda i: (0, 0)),
            pl.BlockSpec((1, d), lambda i: (0, 0)),
        ] + _ffn_specs(d, ffn_w_out.shape[0]),
        out_specs=pl.BlockSpec((tm, d), lambda i: (i, 0)),
        out_shape=jax.ShapeDtypeStruct((n, d), F32),
        compiler_params=_params("parallel"),
        name="outproj_even_ffn",
    )(p, p, y_att, conv_w, w_out, x, g.reshape(1, d), b.reshape(1, d),
      ffn_w_in, ffn_w_out, ffn_g.reshape(1, d), ffn_b.reshape(1, d))


def _mlstm_body(p_ref, gc_ref, gr_ref, bc_ref, br_ref, ng_ref, y_ref, c_scr, n_scr, m_scr):
    L, dh, nh = MLSTM_CHUNK, MLSTM_HEAD_DIM, MLSTM_HEADS

    @pl.when(pl.program_id(1) == 0)
    def _():
        c_scr[...] = jnp.zeros_like(c_scr)
        n_scr[...] = jnp.zeros_like(n_scr)
        m_scr[...] = jnp.zeros_like(m_scr)

    row = lax.broadcasted_iota(jnp.int32, (L, L), 0)
    col = lax.broadcasted_iota(jnp.int32, (L, L), 1)
    causal = col <= row
    tri = causal.astype(BF16)
    tri_t = (row <= col).astype(BF16)

    gcol = gc_ref[...] + bc_ref[...]
    grow = gr_ref[...] + br_ref[...]
    bcum_c = _split_dot(tri, _neg_softplus(-gcol))
    bcum_r = _split_dot_r(_neg_softplus(-grow), tri_t)
    scale = 1.0 / math.sqrt(dh)

    heads = range(nh)
    part = lambda j, h: p_ref[:, j * MLSTM_W + h * dh:j * MLSTM_W + (h + 1) * dh]
    q, k, v = ([part(j, h) for h in heads] for j in range(3))
    rep = lambda a: jnp.broadcast_to(a, (L, LANES))
    i_c = [rep(gcol[:, h:h + 1]) for h in heads]
    b_c = [rep(bcum_c[:, nh + h:nh + h + 1]) for h in heads]
    i_r = [grow[h:h + 1, :] for h in heads]
    b_r = [bcum_r[nh + h:nh + h + 1, :] for h in heads]
    m_prev = [m_scr[h] for h in heads]
    c_prev = [c_scr[h] for h in heads]
    n_prev = [n_scr[h] for h in heads]

    s = [_dot_nt(q[h], k[h]) for h in heads]
    q_c = [_dot(q[h], c_prev[h].astype(BF16)) for h in heads]

    for h in heads:
        b_tot = b_c[h][L - 1:L, :]
        d_state = b_tot - b_c[h] + i_c[h]
        m_new = jnp.maximum(b_tot + m_prev[h], jnp.max(d_state, axis=0, keepdims=True))
        kw = k[h].astype(F32) * (jnp.exp(d_state - m_new) * scale)
        decay = jnp.exp(b_tot + m_prev[h] - m_new)
        c_scr[h] = decay * c_prev[h] + _dot_tn(kw.astype(BF16), v[h])
        n_scr[h] = decay * n_prev[h] + jnp.sum(kw, axis=0, keepdims=True)
        m_scr[h] = m_new

    w_inter, m_t, qk = [], [], []
    for h in heads:
        d_intra = jnp.where(causal, b_c[h] - b_r[h] + i_r[h], NEG_BIG)
        d_inter = b_c[h] + m_prev[h]
        m_t.append(jnp.maximum(d_inter, jnp.max(d_intra, axis=-1, keepdims=True)))
        w_inter.append(jnp.exp(d_inter - m_t[h]))
        qk.append(s[h] * scale * jnp.exp(d_intra - m_t[h]))
    qk_v = [_dot(qk[h].astype(BF16), v[h]) for h in heads]

    for h in heads:
        cols = slice(h * dh, (h + 1) * dh)
        num = w_inter[h] * q_c[h] + qk_v[h]
        den = (w_inter[h] * jnp.sum(q[h].astype(F32) * n_prev[h], axis=-1, keepdims=True)
               + jnp.sum(qk[h], axis=-1, keepdims=True))
        h_tilde = num / jnp.maximum(jnp.abs(den), jnp.exp(-m_t[h]))
        cell = jax.nn.sigmoid(part(3, h).astype(F32)) * h_tilde
        mu = jnp.mean(cell, axis=-1, keepdims=True)
        cc = cell - mu
        var = jnp.mean(cc * cc, axis=-1, keepdims=True)
        y_ref[:, cols] = (cc * lax.rsqrt(var + LN_EPS) * ng_ref[:, cols]).astype(BF16)


def _mlstm(p, gcol, grow, b_i, b_f, norm_g, b, s):
    L = MLSTM_CHUNK
    nc = s // L
    bias = jnp.concatenate([b_i, b_f]).astype(F32)
    bias_c = jnp.zeros((1, LANES), F32).at[0, :GATE_COLS].set(bias)
    bias_r = bias.reshape(GATE_COLS, 1)
    tok = lambda bi, c: (bi * nc + c, 0)
    return pl.pallas_call(
        _mlstm_body,
        grid=(b, nc),
        in_specs=[
            pl.BlockSpec((L, 4 * MLSTM_W), tok),
            pl.BlockSpec((L, LANES), tok),
            pl.BlockSpec((GATE_COLS, L), lambda bi, c: (0, bi * nc + c)),
            pl.BlockSpec((1, LANES), lambda bi, c: (0, 0)),
            pl.BlockSpec((GATE_COLS, 1), lambda bi, c: (0, 0)),
            pl.BlockSpec((1, MLSTM_W), lambda bi, c: (0, 0)),
        ],
        out_specs=pl.BlockSpec((L, MLSTM_W), tok),
        out_shape=jax.ShapeDtypeStruct((b * s, MLSTM_W), BF16),
        scratch_shapes=[
            pltpu.VMEM((MLSTM_HEADS, MLSTM_HEAD_DIM, MLSTM_HEAD_DIM), F32),
            pltpu.VMEM((MLSTM_HEADS, 1, MLSTM_HEAD_DIM), F32),
            pltpu.VMEM((MLSTM_HEADS, 1, LANES), F32),
        ],
        compiler_params=_params("parallel", "arbitrary"),
        name="mlstm",
    )(p, gcol, grow, bias_c, bias_r, norm_g.reshape(1, MLSTM_W))


def _stick_body(bias_ref, q_ref, k_ref, v_ref, y_ref, acc_scr, c_scr):
    t = SB_TILE
    units = range(SB_UNITS)
    qbs = [pl.program_id(2) * SB_UNITS + u for u in units]
    row = lax.broadcasted_iota(jnp.int32, (t, t), 0)
    col = lax.broadcasted_iota(jnp.int32, (t, t), 1)
    suffix = (row >= col).astype(BF16)
    low = lax.broadcasted_iota(jnp.int32, (t, LANES), 1) < SB_HEAD_DIM

    def stacked(q2):
        zero = jnp.zeros_like(q2)
        return jnp.concatenate([jnp.where(low, q2, zero), jnp.where(low, zero, q2)], axis=0)

    def keys(j):
        return pl.ds(pl.multiple_of(j * t, t), t)

    def weights(z, incl, c_later):
        return jnp.exp2(z + incl + c_later).astype(BF16)

    def log_mass(lg):
        return jnp.sum(lg, axis=-1, keepdims=True)

    qs = [stacked(_scaled_q(q_ref[0, u * t:(u + 1) * t, :], LOG2E / math.sqrt(SB_HEAD_DIM))) for u in units]
    j_prev = [jnp.maximum(qb - 1, 0) for qb in qbs]
    causal = bias_ref[...]
    causal = jnp.concatenate([causal, causal], axis=0)
    z_diag = [_dot_nt(qs[u], k_ref[0, keys(qbs[u]), :]) + causal for u in units]
    z_prev = [_dot_nt(qs[u], k_ref[0, keys(j_prev[u]), :]) + jnp.where(qbs[u] > 0, 0.0, NEG_BIG) for u in units]
    lg_diag = [_neg_softplus2(z) for z in z_diag]
    lg_prev = [_neg_softplus2(z) for z in z_prev]
    in_diag = [_split_dot_r(lg, suffix) for lg in lg_diag]
    in_prev = [_split_dot_r(lg, suffix) for lg in lg_prev]
    c_diag = [log_mass(lg) for lg in lg_diag]
    a_diag = [weights(z_diag[u], in_diag[u], 0.0) for u in units]
    a_prev = [weights(z_prev[u], in_prev[u], c_diag[u]) for u in units]
    c_both = [c_diag[u] + log_mass(lg_prev[u]) for u in units]
    for u in units:
        acc_scr[u] = (_dot(a_diag[u], v_ref[0, keys(qbs[u]), :])
                      + _dot(a_prev[u], v_ref[0, keys(j_prev[u]), :]))
        c_scr[u] = jnp.broadcast_to(c_both[u], c_scr.shape[1:])

    def cond(carry):
        j, c_max = carry
        return (j >= 0) & (c_max > SB_LOG2_FLOOR)

    for u in units:
        def body(carry, u=u):
            j, _ = carry
            c_old = c_scr[u]
            z = _dot_nt(qs[u], k_ref[0, keys(j), :])
            lg = _neg_softplus2(z)
            a = weights(z, _split_dot_r(lg, suffix), jnp.concatenate([c_old] * (t // LANES), axis=1))
            acc_scr[u] += _dot(a, v_ref[0, keys(j), :])
            c_new = c_old + log_mass(lg)
            c_scr[u] = c_new
            return j - 1, jnp.max(c_new)

        lax.while_loop(cond, body, (qbs[u] - 2, jnp.max(c_both[u])))
        out = acc_scr[u]
        y_ref[0, u * t:(u + 1) * t, :] = jnp.where(low, out[:t], out[t:]).astype(BF16)


def _stick_breaking(p, b, s, col0):
    t = SB_TILE
    step = SB_UNITS * t
    pairs = SB_W // LANES
    pv = p.reshape(b, s, p.shape[1])
    c0 = col0 // LANES
    idx = np.arange(t)
    causal = jnp.asarray(np.where(idx[None, :] < idx[:, None], 0.0, NEG_BIG), F32)
    return pl.pallas_call(
        _stick_body,
        grid=(b, pairs, s // step),
        in_specs=[
            pl.BlockSpec((t, t), lambda bi, hp, qi: (0, 0)),
            pl.BlockSpec((1, step, LANES), lambda bi, hp, qi: (bi, qi, c0 + hp)),
            pl.BlockSpec((1, s, LANES), lambda bi, hp, qi: (bi, 0, c0 + pairs + hp)),
            pl.BlockSpec((1, s, LANES), lambda bi, hp, qi: (bi, 0, c0 + 2 * pairs + hp)),
        ],
        out_specs=pl.BlockSpec((1, step, LANES), lambda bi, hp, qi: (bi, qi, hp)),
        out_shape=jax.ShapeDtypeStruct((b, s, SB_W), BF16),
        scratch_shapes=[pltpu.VMEM((SB_UNITS, 2 * t, LANES), F32), pltpu.VMEM((SB_UNITS, 2 * t, LANES), F32)],
        compiler_params=_params("parallel", "parallel", "arbitrary"),
        name="stick_breaking",
    )(causal, pv, pv, pv).reshape(b * s, SB_W)


def _outproj_odd_body(ym_ref, ys_ref, wo_ref, x_ref, g_ref, b_ref, win_ref, wout_ref, g3_ref, b3_ref, o_ref):
    mix = _dot(ym_ref[...], wo_ref[:MLSTM_W, :]) + _dot(ys_ref[...], wo_ref[MLSTM_W:, :])
    mixed = _layer_norm(ALPHA * x_ref[...] + mix, g_ref[...], b_ref[...])
    o_ref[...] = _ffn_block(mixed, win_ref, wout_ref, g3_ref[...], b3_ref[...])


def _outproj_odd(y_m, y_s, w_out, x, g, b, ffn):
    n, d = x.shape
    ffn_w_in, ffn_w_out, ffn_g, ffn_b = ffn
    tm = TOKEN_TILE
    return pl.pallas_call(
        _outproj_odd_body,
        grid=(n // tm,),
        in_specs=[
            pl.BlockSpec((tm, MLSTM_W), lambda i: (i, 0)),
            pl.BlockSpec((tm, SB_W), lambda i: (i, 0)),
            pl.BlockSpec((d, d), lambda i: (0, 0), pipeline_mode=pl.Buffered(1)),
            pl.BlockSpec((tm, d), lambda i: (i, 0)),
            pl.BlockSpec((1, d), lambda i: (0, 0)),
            pl.BlockSpec((1, d), lambda i: (0, 0)),
        ] + _ffn_specs(d, ffn_w_out.shape[0]),
        out_specs=pl.BlockSpec((tm, d), lambda i: (i, 0)),
        out_shape=jax.ShapeDtypeStruct((n, d), F32),
        compiler_params=_params("parallel"),
        name="outproj_odd_ffn",
    )(y_m, y_s, w_out, x, g.reshape(1, d), b.reshape(1, d),
      ffn_w_in, ffn_w_out, ffn_g.reshape(1, d), ffn_b.reshape(1, d))


def _even_mixer(x, w_in, conv_w, w_out, g, b, batch, seq, ffn):
    pc, pq = _proj_even(x, w_in.astype(BF16))
    y_att = _dilated_attention(pq, batch, seq)
    return _outproj_even(pc, y_att, conv_w, w_out.astype(BF16), x, g, b, seq, ffn)


def _odd_mixer(x, w_in, b_i, b_f, norm_g, w_out, g, b, batch, seq, ffn):
    gate0 = 4 * MLSTM_W
    w_main = jnp.concatenate([w_in[:, :gate0], w_in[:, gate0 + GATE_COLS:]], axis=1).astype(BF16)
    w_gate = w_in[:, gate0:gate0 + GATE_COLS].astype(BF16)
    w_gate_pad = jnp.pad(w_gate, ((0, 0), (0, LANES - GATE_COLS)))
    p, gcol, grow = _proj_gates(x, w_main, w_gate_pad, w_gate.T)
    y_m = _mlstm(p, gcol, grow, b_i, b_f, norm_g, batch, seq)
    y_s = _stick_breaking(p, batch, seq, gate0)
    return _outproj_odd(y_m, y_s, w_out.astype(BF16), x, g, b, ffn)


def kernel(x, l0_ffn1_w_in, l0_ffn1_w_out, l0_ln1_g, l0_ln1_b, l0_mix_w_in, l0_conv_w, l0_mix_w_out, l0_ln2_g, l0_ln2_b, l0_ffn2_w_in, l0_ffn2_w_out, l0_ln3_g, l0_ln3_b, l1_ffn1_w_in, l1_ffn1_w_out, l1_ln1_g, l1_ln1_b, l1_mix_w_in, l1_mlstm_b_i, l1_mlstm_b_f, l1_mlstm_norm_g, l1_mix_w_out, l1_ln2_g, l1_ln2_b, l1_ffn2_w_in, l1_ffn2_w_out, l1_ln3_g, l1_ln3_b):
    batch, seq, d = x.shape
    ffn = lambda w_in, w_out, g, b: (w_in.astype(BF16), w_out.astype(BF16), g, b)
    t = x.reshape(batch * seq, d)
    t = _ffn_ln(t, ffn(l0_ffn1_w_in, l0_ffn1_w_out, l0_ln1_g, l0_ln1_b))
    t = _even_mixer(t, l0_mix_w_in, l0_conv_w, l0_mix_w_out, l0_ln2_g, l0_ln2_b, batch, seq,
                    ffn(l0_ffn2_w_in, l0_ffn2_w_out, l0_ln3_g, l0_ln3_b))
    t = _ffn_ln(t, ffn(l1_ffn1_w_in, l1_ffn1_w_out, l1_ln1_g, l1_ln1_b))
    t = _odd_mixer(t, l1_mix_w_in, l1_mlstm_b_i, l1_mlstm_b_f, l1_mlstm_norm_g, l1_mix_w_out,
                   l1_ln2_g, l1_ln2_b, batch, seq,
                   ffn(l1_ffn2_w_in, l1_ffn2_w_out, l1_ln3_g, l1_ln3_b))
    return t.reshape(batch, seq, d)
```

```python
import functools
import math

import jax
import jax.numpy as jnp
import numpy as np
from jax import lax
from jax.experimental import pallas as pl
from jax.experimental.pallas import tpu as pltpu

F32 = jnp.float32
BF16 = jnp.bfloat16

DEPTH = 2
ALPHA = (2 * DEPTH) ** 0.25
FFN_RES_WEIGHT = 0.5
LN_EPS = 1e-5

LANES = 128
MXU_TILE = 256
BLK = 128
CONV_CH = 256
CONV_WIDTH = 3
DSWA_HEAD_DIM = 64
DSWA_W = 768
DSWA_SPAN = 128
DSWA_PERIOD = 16
DSWA_SUPER = DSWA_PERIOD * BLK
D1_UNITS = 4
MLSTM_HEADS = 4
MLSTM_W = 512
MLSTM_HEAD_DIM = 128
MLSTM_CHUNK = 128
MLSTM_SEQS = 1
SB_HEAD_DIM = 64
SB_W = 512
GATE_COLS = 2 * MLSTM_HEADS

TOKEN_TILE = 512
SB_TILE = 256
SB_UNITS = 2
SB_LOG2_FLOOR = -127.0
LOG2E = math.log2(math.e)
NEG_BIG = -1e30
VMEM_LIMIT = 56 * 1024 * 1024


def _params(*sem):
    return pltpu.CompilerParams(dimension_semantics=sem, vmem_limit_bytes=VMEM_LIMIT)


def _layer_norm(z, g, b):
    mu = jnp.mean(z, axis=-1, keepdims=True)
    zc = z - mu
    var = jnp.mean(zc * zc, axis=-1, keepdims=True)
    return zc * lax.rsqrt(var + LN_EPS) * g + b


def _dot(a, b):
    return jnp.dot(a, b, preferred_element_type=F32)


def _dot_nt(a, b):
    return lax.dot_general(a, b, (((1,), (1,)), ((), ())), preferred_element_type=F32)


def _dot_tn(a, b):
    return lax.dot_general(a, b, (((0,), (0,)), ((), ())), preferred_element_type=F32)


def _split_dot(tri, x):
    hi = x.astype(BF16)
    lo = (x - hi.astype(F32)).astype(BF16)
    return _dot(tri, hi) + _dot(tri, lo)


def _split_dot_r(x, tri):
    hi = x.astype(BF16)
    lo = (x - hi.astype(F32)).astype(BF16)
    return _dot(hi, tri) + _dot(lo, tri)


def _neg_softplus(z):
    return -(jnp.maximum(z, 0.0) + jnp.log(1.0 + jnp.exp(-jnp.abs(z))))


def _neg_softplus2(z2):
    return -(jnp.maximum(z2, 0.0) + jnp.log(1.0 + jnp.exp2(-jnp.abs(z2))) * LOG2E)


def _scaled_q(q, scale):
    return (q.astype(F32) * scale).astype(BF16)


def _ffn_chunks(d_ff):
    tiles = d_ff // MXU_TILE
    cut = (tiles + 1) // 2 * MXU_TILE
    return ((0, cut), (cut, d_ff))


def _ffn_block(x, win_ref, wout_ref, g, b):
    d_ff = wout_ref.shape[0]
    xb = x.astype(BF16)
    acc = None
    for lo, hi in _ffn_chunks(d_ff):
        gate = _dot(xb, win_ref[:, lo:hi])
        up = _dot(xb, win_ref[:, d_ff + lo:d_ff + hi])
        act = (gate * jax.nn.sigmoid(gate) * up).astype(BF16)
        part = _dot(act, wout_ref[lo:hi, :])
        acc = part if acc is None else acc + part
    return _layer_norm(ALPHA * x + FFN_RES_WEIGHT * acc, g, b)


def _ffn_specs(d, d_ff):
    const = dict(pipeline_mode=pl.Buffered(1))
    return [
        pl.BlockSpec((d, 2 * d_ff), lambda i: (0, 0), **const),
        pl.BlockSpec((d_ff, d), lambda i: (0, 0), **const),
        pl.BlockSpec((1, d), lambda i: (0, 0)),
        pl.BlockSpec((1, d), lambda i: (0, 0)),
    ]


def _ffn_ln_body(x_ref, win_ref, wout_ref, g_ref, b_ref, o_ref):
    o_ref[...] = _ffn_block(x_ref[...], win_ref, wout_ref, g_ref[...], b_ref[...])


def _ffn_ln(x, ffn):
    n, d = x.shape
    w_in, w_out, g, b = ffn
    d_ff = w_out.shape[0]
    tm = TOKEN_TILE
    return pl.pallas_call(
        _ffn_ln_body,
        grid=(n // tm,),
        in_specs=[pl.BlockSpec((tm, d), lambda i: (i, 0))] + _ffn_specs(d, d_ff),
        out_specs=pl.BlockSpec((tm, d), lambda i: (i, 0)),
        out_shape=jax.ShapeDtypeStruct((n, d), F32),
        compiler_params=_params("parallel"),
        name="ffn_ln",
    )(x, w_in, w_out, g.reshape(1, d), b.reshape(1, d))


def _proj_even_body(x_ref, perm_ref, w_ref, pc_ref, pq_ref, *, tm, tn):
    cw = 3 * CONV_CH
    xb = x_ref[...].astype(BF16)
    pc_ref[...] = _dot(xb, w_ref[:, :cw]).astype(BF16)
    xg = _dot(perm_ref[...], xb).astype(BF16)
    rows = tm // DSWA_PERIOD
    for c in range((w_ref.shape[1] - cw) // tn):
        res = _dot(xg, w_ref[:, cw + c * tn:cw + (c + 1) * tn]).astype(BF16)
        pq_ref[0, :, :, c * tn:(c + 1) * tn] = res.reshape(DSWA_PERIOD, rows, tn)


def _proj_even(x, w):
    n, d = x.shape
    cw = 3 * CONV_CH
    qw = w.shape[1] - cw
    tm = TOKEN_TILE
    tiles = DSWA_SUPER // tm
    rows = tm // DSWA_PERIOD
    dst = np.arange(tm)
    perm = np.zeros((tm, tm), np.float32)
    perm[dst, (dst % rows) * DSWA_PERIOD + dst // rows] = 1.0
    const = dict(pipeline_mode=pl.Buffered(1))
    return pl.pallas_call(
        functools.partial(_proj_even_body, tm=tm, tn=768),
        grid=(n // tm,),
        in_specs=[
            pl.BlockSpec((tm, d), lambda i: (i, 0)),
            pl.BlockSpec((tm, tm), lambda i: (0, 0), **const),
            pl.BlockSpec((d, cw + qw), lambda i: (0, 0), **const),
        ],
        out_specs=[
            pl.BlockSpec((tm, cw), lambda i: (i, 0)),
            pl.BlockSpec((1, DSWA_PERIOD, rows, qw), lambda i: (i // tiles, 0, i % tiles, 0)),
        ],
        out_shape=[
            jax.ShapeDtypeStruct((n, cw), BF16),
            jax.ShapeDtypeStruct((n // DSWA_SUPER, DSWA_PERIOD, DSWA_SUPER // DSWA_PERIOD, qw), BF16),
        ],
        compiler_params=_params("parallel"),
        name="proj_even",
    )(x, jnp.asarray(perm, BF16), w)


def _proj_gates_body(x_ref, w_ref, wg_ref, wgt_ref, o_ref, gc_ref, gr_ref, *, tn):
    xb = x_ref[...].astype(BF16)
    for c in range(w_ref.shape[1] // tn):
        o_ref[:, c * tn:(c + 1) * tn] = _dot(xb, w_ref[:, c * tn:(c + 1) * tn]).astype(BF16)
    gc_ref[...] = _dot(xb, wg_ref[...])
    gr_ref[...] = _dot_nt(wgt_ref[...], xb)


def _proj_gates(x, w, wg, wgt):
    n, d = x.shape
    p = w.shape[1]
    tm = TOKEN_TILE
    const = dict(pipeline_mode=pl.Buffered(1))
    return pl.pallas_call(
        functools.partial(_proj_gates_body, tn=512),
        grid=(n // tm,),
        in_specs=[
            pl.BlockSpec((tm, d), lambda i: (i, 0)),
            pl.BlockSpec((d, p), lambda i: (0, 0), **const),
            pl.BlockSpec((d, LANES), lambda i: (0, 0), **const),
            pl.BlockSpec((GATE_COLS, d), lambda i: (0, 0), **const),
        ],
        out_specs=[
            pl.BlockSpec((tm, p), lambda i: (i, 0)),
            pl.BlockSpec((tm, LANES), lambda i: (i, 0)),
            pl.BlockSpec((GATE_COLS, tm), lambda i: (0, i)),
        ],
        out_shape=[
            jax.ShapeDtypeStruct((n, p), BF16),
            jax.ShapeDtypeStruct((n, LANES), F32),
            jax.ShapeDtypeStruct((GATE_COLS, n), F32),
        ],
        compiler_params=_params("parallel"),
        name="proj_odd",
    )(x, w, wg, wgt)


def _dilated_biases():
    def make(q_pos, k_pos, k_is_prev):
        rel = q_pos[:, None] - k_pos[None, :]
        ok = (rel >= 0) & (rel <= DSWA_SPAN)
        variants = [ok, ok & ~k_is_prev[None, :]]
        return jnp.asarray(np.stack([np.where(v, 0.0, NEG_BIG) for v in variants]), F32)

    per = DSWA_PERIOD
    kj = np.arange(2 * BLK)
    b16 = make(np.arange(BLK), kj - BLK, kj < BLK)
    qa, qi = np.divmod(np.arange(BLK), BLK // 4)
    ka, ki = np.divmod(np.arange(2 * BLK), 2 * BLK // 4)
    b4 = make(4 * qi + qa, 4 * (ki - BLK // 4) + ka, ki < BLK // 4)
    qg, qi = np.divmod(np.arange(BLK), BLK // per)
    kg, ki = np.divmod(np.arange(2 * BLK), 2 * BLK // per)
    b1 = make(per * qi + qg, per * (ki - BLK // per) + kg, ki < BLK // per)
    return b16, b4, b1


def _dilated_body(b16_ref, b4_ref, b1_ref, q_ref, kp_ref, kc_ref, vp_ref, vc_ref, y_ref,
                  kf, vf, acc, m_s, l_s, nat, q32, k32, v32):
    per = DSWA_PERIOD
    first = jnp.where(pl.program_id(1) == 0, 1, 0)
    kf[:, :BLK, :] = kp_ref[0]
    kf[:, BLK:, :] = kc_ref[0]
    vf[:, :BLK, :] = vp_ref[0]
    vf[:, BLK:, :] = vc_ref[0]
    scale = LOG2E / math.sqrt(DSWA_HEAD_DIM)

    def low_mask(rows):
        return lax.broadcasted_iota(jnp.int32, (rows, LANES), 1) < DSWA_HEAD_DIM

    def attend(units):
        scores = []
        for q2, kk, _, _ in units:
            low = low_mask(q2.shape[0])
            zero = jnp.zeros_like(q2)
            stacked = jnp.concatenate([jnp.where(low, q2, zero), jnp.where(low, zero, q2)], axis=0)
            scores.append(_dot_nt(stacked, kk))
        soft = []
        for s, (_, _, _, bias) in zip(scores, units):
            s = s + jnp.concatenate([bias, bias], axis=0)
            m = jnp.max(s, axis=-1, keepdims=True)
            p = jnp.exp2(s - m)
            soft.append((m, jnp.sum(p, axis=-1, keepdims=True), p.astype(BF16)))
        return [(m, l, _dot(p, vv)) for (m, l, p), (_, _, vv, _) in zip(soft, units)]

    def by_lane(x):
        r = x.shape[0] // 2
        return jnp.where(low_mask(r), x[:r], x[r:])

    def merge(part, m_old, l_old, acc_old):
        m, l, pv = (by_lane(x) for x in part)
        m_tot = jnp.maximum(m_old, m)
        a_old = jnp.exp2(m_old - m_tot)
        a_new = jnp.exp2(m - m_tot)
        return m_tot, l_old * a_old + l * a_new, acc_old * a_old + pv * a_new

    def gather(ref, lead, groups, rows):
        return jnp.concatenate([ref[lead + (g, rows, slice(None))] for g in groups], axis=0)

    def scatter(ref, groups, rows, value):
        n = value.shape[0] // len(groups)
        for j, g in enumerate(groups):
            ref[g, rows, :] = value[j * n:(j + 1) * n]

    state = (m_s, l_s, acc)

    def fold(parts, where):
        return [merge(part, *(gather(ref, (), groups, rows) for ref in state))
                for part, (groups, rows) in zip(parts, where)]

    def d16(it, carry):
        groups = [8 * it + j for j in range(8)]
        parts = attend([(_scaled_q(q_ref[0, g], scale), kf[g], vf[g], b16_ref[first]) for g in groups])
        for g, part in zip(groups, parts):
            for ref, x in zip(state, part):
                ref[g] = by_lane(x)
        return carry

    lax.fori_loop(0, per // 8, d16, 0)

    def d4(r4, carry):
        nq = BLK // 4
        groups = [4 * a + r4 for a in range(4)]
        units, where = [], []
        for blk in range(4):
            qrows = slice(blk * nq, (blk + 1) * nq)
            krows = slice(BLK - nq + blk * nq, BLK + (blk + 1) * nq)
            bias = b4_ref[first] if blk == 0 else b4_ref[0]
            units.append((_scaled_q(gather(q_ref, (0,), groups, qrows), scale),
                          gather(kf, (), groups, krows), gather(vf, (), groups, krows), bias))
            where.append((groups, qrows))
        for merged, (groups, rows) in zip(fold(attend(units), where), where):
            for ref, x in zip(state, merged):
                scatter(ref, groups, rows, x)
        return carry

    lax.fori_loop(0, 4, d4, 0)

    q32[...] = q_ref[0].astype(F32)
    k32[...] = kf[...].astype(F32)
    v32[...] = vf[...].astype(F32)

    def d1(it, carry):
        nq = BLK // per
        groups = list(range(per))
        units, where, blks = [], [], []
        for j in range(D1_UNITS):
            blk = D1_UNITS * it + j
            qrows = pl.ds(pl.multiple_of(blk * nq, nq), nq)
            krows = pl.ds(pl.multiple_of(BLK - nq + blk * nq, nq), 2 * nq)
            flag = jnp.where(blk == 0, first, 0)
            units.append((_scaled_q(gather(q32, (), groups, qrows), scale),
                          gather(k32, (), groups, krows).astype(BF16),
                          gather(v32, (), groups, krows).astype(BF16), b1_ref[flag]))
            where.append((groups, qrows))
            blks.append(blk)
        for j, (blk, (_, l, a)) in enumerate(zip(blks, fold(attend(units), where))):
            r = a.shape[0]
            out = a / l
            for g in groups:
                nat[j, pl.ds(g, nq, stride=per), :] = out[g * nq:(g + 1) * nq]
            y_ref[0, pl.ds(pl.multiple_of(blk * r, r), r), :] = nat[j].astype(BF16)
        return carry

    lax.fori_loop(0, DSWA_SUPER // (D1_UNITS * BLK), d1, 0)


def _dilated_attention(pq, b, s):
    per, w = DSWA_PERIOD, DSWA_W
    pairs = w // LANES
    nsb = s // DSWA_SUPER
    rows = DSWA_SUPER // per
    biases = _dilated_biases()
    blk = (1, per, rows, LANES)
    cur = lambda col: (lambda bi, sb, hp: (bi * nsb + sb, 0, 0, col * pairs + hp))
    prev = lambda col: (lambda bi, sb, hp: (bi * nsb + jnp.maximum(sb - 1, 0), 0, 0, col * pairs + hp))
    const = lambda a: pl.BlockSpec(a.shape, lambda bi, sb, hp: (0, 0, 0))
    return pl.pallas_call(
        _dilated_body,
        grid=(b, nsb, pairs),
        in_specs=[const(a) for a in biases] + [
            pl.BlockSpec(blk, cur(0)),
            pl.BlockSpec(blk, prev(1)),
            pl.BlockSpec(blk, cur(1)),
            pl.BlockSpec(blk, prev(2)),
            pl.BlockSpec(blk, cur(2)),
        ],
        out_specs=pl.BlockSpec((1, DSWA_SUPER, LANES), lambda bi, sb, hp: (bi * nsb + sb, 0, hp)),
        out_shape=jax.ShapeDtypeStruct((b * nsb, DSWA_SUPER, w), BF16),
        scratch_shapes=[
            pltpu.VMEM((per, 2 * rows, LANES), BF16),
            pltpu.VMEM((per, 2 * rows, LANES), BF16),
            pltpu.VMEM((per, rows, LANES), F32),
            pltpu.VMEM((per, rows, LANES), F32),
            pltpu.VMEM((per, rows, LANES), F32),
            pltpu.VMEM((D1_UNITS, BLK, LANES), F32),
            pltpu.VMEM((per, rows, LANES), F32),
            pltpu.VMEM((per, 2 * rows, LANES), F32),
            pltpu.VMEM((per, 2 * rows, LANES), F32),
        ],
        compiler_params=_params("parallel", "parallel", "parallel"),
        name="dilated",
    )(*biases, pq, pq, pq, pq, pq).reshape(b * s, w)


def _outproj_even_body(p_ref, halo_ref, y_ref, cw_ref, wo_ref, x_ref, g_ref, b_ref,
                       win_ref, wout_ref, g3_ref, b3_ref, o_ref, *, seq, tm, halo):
    c = CONV_CH
    pc = p_ref[...].astype(F32)
    bg, u = pc[:, :c], pc[:, c:2 * c] * pc[:, 2 * c:3 * c]
    ph = halo_ref[...].astype(F32)
    uh = ph[:, c:2 * c] * ph[:, 2 * c:3 * c]
    starts_sequence = (pl.program_id(0) * tm) % seq == 0
    uh = jnp.where(starts_sequence, 0.0, uh)
    full = jnp.concatenate([uh, u], axis=0)
    cw = cw_ref[...]
    conv = cw[2:3, :] * u + cw[1:2, :] * full[halo - 1:halo - 1 + tm] + cw[0:1, :] * full[halo - 2:halo - 2 + tm]
    y_conv = (bg * conv).astype(BF16)
    mix = _dot(y_conv, wo_ref[:c, :]) + _dot(y_ref[...], wo_ref[c:, :])
    mixed = _layer_norm(ALPHA * x_ref[...] + mix, g_ref[...], b_ref[...])
    o_ref[...] = _ffn_block(mixed, win_ref, wout_ref, g3_ref[...], b3_ref[...])


def _outproj_even(p, y_att, conv_w, w_out, x, g, b, seq, ffn):
    n, d = x.shape
    ffn_w_in, ffn_w_out, ffn_g, ffn_b = ffn
    tm = TOKEN_TILE
    halo = 16
    cw = 3 * CONV_CH
    return pl.pallas_call(
        functools.partial(_outproj_even_body, seq=seq, tm=tm, halo=halo),
        grid=(n // tm,),
        in_specs=[
            pl.BlockSpec((tm, cw), lambda i: (i, 0)),
            pl.BlockSpec((halo, cw), lambda i: (jnp.maximum(i * (tm // halo) - 1, 0), 0)),
            pl.BlockSpec((tm, DSWA_W), lambda i: (i, 0)),
            pl.BlockSpec((CONV_WIDTH, CONV_CH), lambda i: (0, 0)),
            pl.BlockSpec((d, d), lambda i: (0, 0), pipeline_mode=pl.Buffered(1)),
            pl.BlockSpec((tm, d), lambda i: (i, 0)),
            pl.BlockSpec((1, d), lambda i: (0, 0)),
            pl.BlockSpec((1, d), lambda i: (0, 0)),
        ] + _ffn_specs(d, ffn_w_out.shape[0]),
        out_specs=pl.BlockSpec((tm, d), lambda i: (i, 0)),
        out_shape=jax.ShapeDtypeStruct((n, d), F32),
        compiler_params=_params("parallel"),
        name="outproj_even_ffn",
    )(p, p, y_att, conv_w, w_out, x, g.reshape(1, d), b.reshape(1, d),
      ffn_w_in, ffn_w_out, ffn_g.reshape(1, d), ffn_b.reshape(1, d))


def _mlstm_body(*refs):
    L, dh, nh, ns = MLSTM_CHUNK, MLSTM_HEAD_DIM, MLSTM_HEADS, MLSTM_SEQS
    p_refs, gc_refs, gr_refs = refs[:ns], refs[ns:2 * ns], refs[2 * ns:3 * ns]
    bc_ref, br_ref, ng_ref, y_ref, c_scr, n_scr, m_scr = refs[3 * ns:]

    @pl.when(pl.program_id(1) == 0)
    def _():
        c_scr[...] = jnp.zeros_like(c_scr)
        n_scr[...] = jnp.zeros_like(n_scr)
        m_scr[...] = jnp.zeros_like(m_scr)

    row = lax.broadcasted_iota(jnp.int32, (L, L), 0)
    col = lax.broadcasted_iota(jnp.int32, (L, L), 1)
    causal = col <= row
    tri = causal.astype(BF16)
    tri_t = (row <= col).astype(BF16)

    scale = 1.0 / math.sqrt(dh)
    rep = lambda a: jnp.broadcast_to(a, (L, LANES))

    chains = [(j, h) for j in range(ns) for h in range(nh)]
    ids = range(len(chains))

    def part(i, w):
        j, h = chains[i]
        return p_refs[j][:, w * MLSTM_W + h * dh:w * MLSTM_W + (h + 1) * dh]

    q, k, v = ([part(i, w) for i in ids] for w in range(3))
    gcol = [gc_refs[j][...] + bc_ref[...] for j in range(ns)]
    grow = [gr_refs[j][...] + br_ref[...] for j in range(ns)]
    bcum_c = [_split_dot(tri, _neg_softplus(-g)) for g in gcol]
    bcum_r = [_split_dot_r(_neg_softplus(-g), tri_t) for g in grow]
    i_c = [rep(gcol[j][:, h:h + 1]) for j, h in chains]
    b_c = [rep(bcum_c[j][:, nh + h:nh + h + 1]) for j, h in chains]
    i_r = [grow[j][h:h + 1, :] for j, h in chains]
    b_r = [bcum_r[j][nh + h:nh + h + 1, :] for j, h in chains]
    m_prev = [m_scr[i] for i in ids]
    c_prev = [c_scr[i] for i in ids]
    n_prev = [n_scr[i] for i in ids]

    s = [_dot_nt(q[i], k[i]) for i in ids]
    q_c = [_dot(q[i], c_prev[i].astype(BF16)) for i in ids]

    for i in ids:
        b_tot = b_c[i][L - 1:L, :]
        d_state = b_tot - b_c[i] + i_c[i]
        m_new = jnp.maximum(b_tot + m_prev[i], jnp.max(d_state, axis=0, keepdims=True))
        kw = k[i].astype(F32) * (jnp.exp(d_state - m_new) * scale)
        decay = jnp.exp(b_tot + m_prev[i] - m_new)
        c_scr[i] = decay * c_prev[i] + _dot_tn(kw.astype(BF16), v[i])
        n_scr[i] = decay * n_prev[i] + jnp.sum(kw, axis=0, keepdims=True)
        m_scr[i] = m_new

    w_inter, m_t, qk = [], [], []
    for i in ids:
        d_intra = jnp.where(causal, b_c[i] - b_r[i] + i_r[i], NEG_BIG)
        d_inter = b_c[i] + m_prev[i]
        m_t.append(jnp.maximum(d_inter, jnp.max(d_intra, axis=-1, keepdims=True)))
        w_inter.append(jnp.exp(d_inter - m_t[i]))
        qk.append(s[i] * scale * jnp.exp(d_intra - m_t[i]))
    qk_v = [_dot(qk[i].astype(BF16), v[i]) for i in ids]

    for i, (j, h) in enumerate(chains):
        cols = slice(h * dh, (h + 1) * dh)
        num = w_inter[i] * q_c[i] + qk_v[i]
        den = (w_inter[i] * jnp.sum(q[i].astype(F32) * n_prev[i], axis=-1, keepdims=True)
               + jnp.sum(qk[i], axis=-1, keepdims=True))
        h_tilde = num / jnp.maximum(jnp.abs(den), jnp.exp(-m_t[i]))
        cell = jax.nn.sigmoid(part(i, 3).astype(F32)) * h_tilde
        mu = jnp.mean(cell, axis=-1, keepdims=True)
        cc = cell - mu
        var = jnp.mean(cc * cc, axis=-1, keepdims=True)
        y_ref[0, j, :, cols] = (cc * lax.rsqrt(var + LN_EPS) * ng_ref[:, cols]).astype(BF16)


def _mlstm(p, gcol, grow, b_i, b_f, norm_g, b, s):
    L, ns = MLSTM_CHUNK, MLSTM_SEQS
    nc = s // L
    chains = ns * MLSTM_HEADS
    bias = jnp.concatenate([b_i, b_f]).astype(F32)
    bias_c = jnp.zeros((1, LANES), F32).at[0, :GATE_COLS].set(bias)
    bias_r = bias.reshape(GATE_COLS, 1)
    tok = lambda j: (lambda bi, c: ((bi * ns + j) * nc + c, 0))
    tok_t = lambda j: (lambda bi, c: (0, (bi * ns + j) * nc + c))
    const = lambda bi, c: (0, 0)
    return pl.pallas_call(
        _mlstm_body,
        grid=(b // ns, nc),
        in_specs=(
            [pl.BlockSpec((L, 4 * MLSTM_W), tok(j)) for j in range(ns)]
            + [pl.BlockSpec((L, LANES), tok(j)) for j in range(ns)]
            + [pl.BlockSpec((GATE_COLS, L), tok_t(j)) for j in range(ns)]
            + [pl.BlockSpec((1, LANES), const), pl.BlockSpec((GATE_COLS, 1), const),
               pl.BlockSpec((1, MLSTM_W), const)]),
        out_specs=pl.BlockSpec((1, ns, L, MLSTM_W), lambda bi, c: (bi, 0, c, 0)),
        out_shape=jax.ShapeDtypeStruct((b // ns, ns, s, MLSTM_W), BF16),
        scratch_shapes=[
            pltpu.VMEM((chains, MLSTM_HEAD_DIM, MLSTM_HEAD_DIM), F32),
            pltpu.VMEM((chains, 1, MLSTM_HEAD_DIM), F32),
            pltpu.VMEM((chains, 1, LANES), F32),
        ],
        compiler_params=_params("parallel", "arbitrary"),
        name="mlstm",
    )(*([p] * ns + [gcol] * ns + [grow] * ns), bias_c, bias_r,
      norm_g.reshape(1, MLSTM_W)).reshape(b * s, MLSTM_W)


def _stick_body(bias_ref, q_ref, k_ref, v_ref, y_ref, acc_scr, c_scr):
    t = SB_TILE
    units = range(SB_UNITS)
    qbs = [pl.program_id(2) * SB_UNITS + u for u in units]
    row = lax.broadcasted_iota(jnp.int32, (t, t), 0)
    col = lax.broadcasted_iota(jnp.int32, (t, t), 1)
    suffix = (row >= col).astype(BF16)
    low = lax.broadcasted_iota(jnp.int32, (t, LANES), 1) < SB_HEAD_DIM

    def stacked(q2):
        zero = jnp.zeros_like(q2)
        return jnp.concatenate([jnp.where(low, q2, zero), jnp.where(low, zero, q2)], axis=0)

    def keys(j):
        return pl.ds(pl.multiple_of(j * t, t), t)

    def weights(z, incl, c_later):
        return jnp.exp2(z + incl + c_later).astype(BF16)

    def log_mass(lg):
        return jnp.sum(lg, axis=-1, keepdims=True)

    qs = [stacked(_scaled_q(q_ref[0, u * t:(u + 1) * t, :], LOG2E / math.sqrt(SB_HEAD_DIM))) for u in units]
    j_prev = [jnp.maximum(qb - 1, 0) for qb in qbs]
    causal = bias_ref[...]
    causal = jnp.concatenate([causal, causal], axis=0)
    z_diag = [_dot_nt(qs[u], k_ref[0, keys(qbs[u]), :]) + causal for u in units]
    z_prev = [_dot_nt(qs[u], k_ref[0, keys(j_prev[u]), :]) + jnp.where(qbs[u] > 0, 0.0, NEG_BIG) for u in units]
    lg_diag = [_neg_softplus2(z) for z in z_diag]
    lg_prev = [_neg_softplus2(z) for z in z_prev]
    in_diag = [_split_dot_r(lg, suffix) for lg in lg_diag]
    in_prev = [_split_dot_r(lg, suffix) for lg in lg_prev]
    c_diag = [log_mass(lg) for lg in lg_diag]
    a_diag = [weights(z_diag[u], in_diag[u], 0.0) for u in units]
    a_prev = [weights(z_prev[u], in_prev[u], c_diag[u]) for u in units]
    c_both = [c_diag[u] + log_mass(lg_prev[u]) for u in units]
    for u in units:
        acc_scr[u] = (_dot(a_diag[u], v_ref[0, keys(qbs[u]), :])
                      + _dot(a_prev[u], v_ref[0, keys(j_prev[u]), :]))
        c_scr[u] = jnp.broadcast_to(c_both[u], c_scr.shape[1:])

    def cond(carry):
        j, c_max = carry
        return (j >= 0) & (c_max > SB_LOG2_FLOOR)

    for u in units:
        def body(carry, u=u):
            j, _ = carry
            c_old = c_scr[u]
            z = _dot_nt(qs[u], k_ref[0, keys(j), :])
            lg = _neg_softplus2(z)
            a = weights(z, _split_dot_r(lg, suffix), jnp.concatenate([c_old] * (t // LANES), axis=1))
            acc_scr[u] += _dot(a, v_ref[0, keys(j), :])
            c_new = c_old + log_mass(lg)
            c_scr[u] = c_new
            return j - 1, jnp.max(c_new)

        lax.while_loop(cond, body, (qbs[u] - 2, jnp.max(c_both[u])))
        out = acc_scr[u]
        y_ref[0, u * t:(u + 1) * t, :] = jnp.where(low, out[:t], out[t:]).astype(BF16)


def _stick_breaking(p, b, s, col0):
    t = SB_TILE
    step = SB_UNITS * t
    pairs = SB_W // LANES
    pv = p.reshape(b, s, p.shape[1])
    c0 = col0 // LANES
    idx = np.arange(t)
    causal = jnp.asarray(np.where(idx[None, :] < idx[:, None], 0.0, NEG_BIG), F32)
    return pl.pallas_call(
        _stick_body,
        grid=(b, pairs, s // step),
        in_specs=[
            pl.BlockSpec((t, t), lambda bi, hp, qi: (0, 0)),
            pl.BlockSpec((1, step, LANES), lambda bi, hp, qi: (bi, qi, c0 + hp)),
            pl.BlockSpec((1, s, LANES), lambda bi, hp, qi: (bi, 0, c0 + pairs + hp)),
            pl.BlockSpec((1, s, LANES), lambda bi, hp, qi: (bi, 0, c0 + 2 * pairs + hp)),
        ],
        out_specs=pl.BlockSpec((1, step, LANES), lambda bi, hp, qi: (bi, qi, hp)),
        out_shape=jax.ShapeDtypeStruct((b, s, SB_W), BF16),
        scratch_shapes=[pltpu.VMEM((SB_UNITS, 2 * t, LANES), F32), pltpu.VMEM((SB_UNITS, 2 * t, LANES), F32)],
        compiler_params=_params("parallel", "parallel", "arbitrary"),
        name="stick_breaking",
    )(causal, pv, pv, pv).reshape(b * s, SB_W)


def _outproj_odd_body(ym_ref, ys_ref, wo_ref, x_ref, g_ref, b_ref, win_ref, wout_ref, g3_ref, b3_ref, o_ref):
    mix = _dot(ym_ref[...], wo_ref[:MLSTM_W, :]) + _dot(ys_ref[...], wo_ref[MLSTM_W:, :])
    mixed = _layer_norm(ALPHA * x_ref[...] + mix, g_ref[...], b_ref[...])
    o_ref[...] = _ffn_block(mixed, win_ref, wout_ref, g3_ref[...], b3_ref[...])


def _outproj_odd(y_m, y_s, w_out, x, g, b, ffn):
    n, d = x.shape
    ffn_w_in, ffn_w_out, ffn_g, ffn_b = ffn
    tm = TOKEN_TILE
    return pl.pallas_call(
        _outproj_odd_body,
        grid=(n // tm,),
        in_specs=[
            pl.BlockSpec((tm, MLSTM_W), lambda i: (i, 0)),
            pl.BlockSpec((tm, SB_W), lambda i: (i, 0)),
            pl.BlockSpec((d, d), lambda i: (0, 0), pipeline_mode=pl.Buffered(1)),
            pl.BlockSpec((tm, d), lambda i: (i, 0)),
            pl.BlockSpec((1, d), lambda i: (0, 0)),
            pl.BlockSpec((1, d), lambda i: (0, 0)),
        ] + _ffn_specs(d, ffn_w_out.shape[0]),
        out_specs=pl.BlockSpec((tm, d), lambda i: (i, 0)),
        out_shape=jax.ShapeDtypeStruct((n, d), F32),
        compiler_params=_params("parallel"),
        name="outproj_odd_ffn",
    )(y_m, y_s, w_out, x, g.reshape(1, d), b.reshape(1, d),
      ffn_w_in, ffn_w_out, ffn_g.reshape(1, d), ffn_b.reshape(1, d))


def _even_mixer(x, w_in, conv_w, w_out, g, b, batch, seq, ffn):
    pc, pq = _proj_even(x, w_in.astype(BF16))
    y_att = _dilated_attention(pq, batch, seq)
    return _outproj_even(pc, y_att, conv_w, w_out.astype(BF16), x, g, b, seq, ffn)


def _odd_mixer(x, w_in, b_i, b_f, norm_g, w_out, g, b, batch, seq, ffn):
    gate0 = 4 * MLSTM_W
    w_main = jnp.concatenate([w_in[:, :gate0], w_in[:, gate0 + GATE_COLS:]], axis=1).astype(BF16)
    w_gate = w_in[:, gate0:gate0 + GATE_COLS].astype(BF16)
    w_gate_pad = jnp.pad(w_gate, ((0, 0), (0, LANES - GATE_COLS)))
    p, gcol, grow = _proj_gates(x, w_main, w_gate_pad, w_gate.T)
    y_m = _mlstm(p, gcol, grow, b_i, b_f, norm_g, batch, seq)
    y_s = _stick_breaking(p, batch, seq, gate0)
    return _outproj_odd(y_m, y_s, w_out.astype(BF16), x, g, b, ffn)


def kernel(x, l0_ffn1_w_in, l0_ffn1_w_out, l0_ln1_g, l0_ln1_b, l0_mix_w_in, l0_conv_w, l0_mix_w_out, l0_ln2_g, l0_ln2_b, l0_ffn2_w_in, l0_ffn2_w_out, l0_ln3_g, l0_ln3_b, l1_ffn1_w_in, l1_ffn1_w_out, l1_ln1_g, l1_ln1_b, l1_mix_w_in, l1_mlstm_b_i, l1_mlstm_b_f, l1_mlstm_norm_g, l1_mix_w_out, l1_ln2_g, l1_ln2_b, l1_ffn2_w_in, l1_ffn2_w_out, l1_ln3_g, l1_ln3_b):
    batch, seq, d = x.shape
    ffn = lambda w_in, w_out, g, b: (w_in.astype(BF16), w_out.astype(BF16), g, b)
    t = x.reshape(batch * seq, d)
    t = _ffn_ln(t, ffn(l0_ffn1_w_in, l0_ffn1_w_out, l0_ln1_g, l0_ln1_b))
    t = _even_mixer(t, l0_mix_w_in, l0_conv_w, l0_mix_w_out, l0_ln2_g, l0_ln2_b, batch, seq,
                    ffn(l0_ffn2_w_in, l0_ffn2_w_out, l0_ln3_g, l0_ln3_b))
    t = _ffn_ln(t, ffn(l1_ffn1_w_in, l1_ffn1_w_out, l1_ln1_g, l1_ln1_b))
    t = _odd_mixer(t, l1_mix_w_in, l1_mlstm_b_i, l1_mlstm_b_f, l1_mlstm_norm_g, l1_mix_w_out,
                   l1_ln2_g, l1_ln2_b, batch, seq,
                   ffn(l1_ffn2_w_in, l1_ffn2_w_out, l1_ln3_g, l1_ln3_b))
    return t.reshape(batch, seq, d)
```

```python
import functools
import math

import jax
import jax.numpy as jnp
import numpy as np
from jax import lax
from jax.experimental import pallas as pl
from jax.experimental.pallas import tpu as pltpu

F32 = jnp.float32
BF16 = jnp.bfloat16

DEPTH = 2
ALPHA = (2 * DEPTH) ** 0.25
FFN_RES_WEIGHT = 0.5
LN_EPS = 1e-5

LANES = 128
MXU_TILE = 256
BLK = 128
CONV_CH = 256
CONV_WIDTH = 3
DSWA_HEAD_DIM = 64
DSWA_W = 768
DSWA_SPAN = 128
DSWA_PERIOD = 16
DSWA_SUPER = DSWA_PERIOD * BLK
D1_UNITS = 4
MLSTM_HEADS = 4
MLSTM_W = 512
MLSTM_HEAD_DIM = 128
MLSTM_CHUNK = 128
MLSTM_SEQS = 1
SB_HEAD_DIM = 64
SB_W = 512
GATE_COLS = 2 * MLSTM_HEADS

TOKEN_TILE = 512
SB_TILE = 128
SB_WINDOW = 256
SB_UNITS = 4
SB_LOG2_FLOOR = -127.0
LOG2E = math.log2(math.e)
NEG_BIG = -1e30
VMEM_LIMIT = 56 * 1024 * 1024


def _params(*sem):
    return pltpu.CompilerParams(dimension_semantics=sem, vmem_limit_bytes=VMEM_LIMIT)


def _layer_norm(z, g, b):
    mu = jnp.mean(z, axis=-1, keepdims=True)
    zc = z - mu
    var = jnp.mean(zc * zc, axis=-1, keepdims=True)
    return zc * lax.rsqrt(var + LN_EPS) * g + b


def _dot(a, b):
    return jnp.dot(a, b, preferred_element_type=F32)


def _dot_nt(a, b):
    return lax.dot_general(a, b, (((1,), (1,)), ((), ())), preferred_element_type=F32)


def _dot_tn(a, b):
    return lax.dot_general(a, b, (((0,), (0,)), ((), ())), preferred_element_type=F32)


def _split_dot(tri, x):
    hi = x.astype(BF16)
    lo = (x - hi.astype(F32)).astype(BF16)
    return _dot(tri, hi) + _dot(tri, lo)


def _split_dot_r(x, tri):
    hi = x.astype(BF16)
    lo = (x - hi.astype(F32)).astype(BF16)
    return _dot(hi, tri) + _dot(lo, tri)


def _neg_softplus(z):
    return -(jnp.maximum(z, 0.0) + jnp.log(1.0 + jnp.exp(-jnp.abs(z))))


def _neg_softplus2(z2):
    return -(jnp.maximum(z2, 0.0) + jnp.log(1.0 + jnp.exp2(-jnp.abs(z2))) * LOG2E)


def _scaled_q(q, scale):
    return (q.astype(F32) * scale).astype(BF16)


def _ffn_chunks(d_ff):
    tiles = d_ff // MXU_TILE
    cut = (tiles + 1) // 2 * MXU_TILE
    return ((0, cut), (cut, d_ff))


def _ffn_block(x, win_ref, wout_ref, g, b):
    d_ff = wout_ref.shape[0]
    halves = range(2)
    r = x.shape[0] // 2
    xs = [x[h * r:(h + 1) * r] for h in halves]
    xb = [t.astype(BF16) for t in xs]
    acc = [None, None]
    for lo, hi in _ffn_chunks(d_ff):
        gate_up = [(_dot(xb[h], win_ref[:, lo:hi]), _dot(xb[h], win_ref[:, d_ff + lo:d_ff + hi])) for h in halves]
        for h in halves:
            gate, up = gate_up[h]
            act = (gate * jax.nn.sigmoid(gate) * up).astype(BF16)
            part = _dot(act, wout_ref[lo:hi, :])
            acc[h] = part if acc[h] is None else acc[h] + part
    return jnp.concatenate([_layer_norm(ALPHA * xs[h] + FFN_RES_WEIGHT * acc[h], g, b) for h in halves], axis=0)


def _ffn_specs(d, d_ff):
    const = dict(pipeline_mode=pl.Buffered(1))
    return [
        pl.BlockSpec((d, 2 * d_ff), lambda i: (0, 0), **const),
        pl.BlockSpec((d_ff, d), lambda i: (0, 0), **const),
        pl.BlockSpec((1, d), lambda i: (0, 0)),
        pl.BlockSpec((1, d), lambda i: (0, 0)),
    ]


def _ffn_ln_body(x_ref, win_ref, wout_ref, g_ref, b_ref, o_ref):
    o_ref[...] = _ffn_block(x_ref[...], win_ref, wout_ref, g_ref[...], b_ref[...])


def _ffn_ln(x, ffn):
    n, d = x.shape
    w_in, w_out, g, b = ffn
    d_ff = w_out.shape[0]
    tm = TOKEN_TILE
    return pl.pallas_call(
        _ffn_ln_body,
        grid=(n // tm,),
        in_specs=[pl.BlockSpec((tm, d), lambda i: (i, 0))] + _ffn_specs(d, d_ff),
        out_specs=pl.BlockSpec((tm, d), lambda i: (i, 0)),
        out_shape=jax.ShapeDtypeStruct((n, d), F32),
        compiler_params=_params("parallel"),
        name="ffn_ln",
    )(x, w_in, w_out, g.reshape(1, d), b.reshape(1, d))


def _proj_even_body(x_ref, perm_ref, w_ref, pc_ref, pq_ref, *, tm, tn):
    cw = 3 * CONV_CH
    xb = x_ref[...].astype(BF16)
    pc_ref[...] = _dot(xb, w_ref[:, :cw]).astype(BF16)
    xg = _dot(perm_ref[...], xb).astype(BF16)
    rows = tm // DSWA_PERIOD
    for c in range((w_ref.shape[1] - cw) // tn):
        res = _dot(xg, w_ref[:, cw + c * tn:cw + (c + 1) * tn]).astype(BF16)
        pq_ref[0, :, :, c * tn:(c + 1) * tn] = res.reshape(DSWA_PERIOD, rows, tn)


def _proj_even(x, w):
    n, d = x.shape
    cw = 3 * CONV_CH
    qw = w.shape[1] - cw
    tm = TOKEN_TILE
    tiles = DSWA_SUPER // tm
    rows = tm // DSWA_PERIOD
    dst = np.arange(tm)
    perm = np.zeros((tm, tm), np.float32)
    perm[dst, (dst % rows) * DSWA_PERIOD + dst // rows] = 1.0
    const = dict(pipeline_mode=pl.Buffered(1))
    return pl.pallas_call(
        functools.partial(_proj_even_body, tm=tm, tn=768),
        grid=(n // tm,),
        in_specs=[
            pl.BlockSpec((tm, d), lambda i: (i, 0)),
            pl.BlockSpec((tm, tm), lambda i: (0, 0), **const),
            pl.BlockSpec((d, cw + qw), lambda i: (0, 0), **const),
        ],
        out_specs=[
            pl.BlockSpec((tm, cw), lambda i: (i, 0)),
            pl.BlockSpec((1, DSWA_PERIOD, rows, qw), lambda i: (i // tiles, 0, i % tiles, 0)),
        ],
        out_shape=[
            jax.ShapeDtypeStruct((n, cw), BF16),
            jax.ShapeDtypeStruct((n // DSWA_SUPER, DSWA_PERIOD, DSWA_SUPER // DSWA_PERIOD, qw), BF16),
        ],
        compiler_params=_params("parallel"),
        name="proj_even",
    )(x, jnp.asarray(perm, BF16), w)


def _proj_gates_body(x_ref, w_ref, wg_ref, wgt_ref, o_ref, gc_ref, gr_ref, *, tn):
    xb = x_ref[...].astype(BF16)
    for c in range(w_ref.shape[1] // tn):
        o_ref[:, c * tn:(c + 1) * tn] = _dot(xb, w_ref[:, c * tn:(c + 1) * tn]).astype(BF16)
    gc_ref[...] = _dot(xb, wg_ref[...])
    gr_ref[...] = _dot_nt(wgt_ref[...], xb)


def _proj_gates(x, w, wg, wgt):
    n, d = x.shape
    p = w.shape[1]
    tm = TOKEN_TILE
    const = dict(pipeline_mode=pl.Buffered(1))
    return pl.pallas_call(
        functools.partial(_proj_gates_body, tn=512),
        grid=(n // tm,),
        in_specs=[
            pl.BlockSpec((tm, d), lambda i: (i, 0)),
            pl.BlockSpec((d, p), lambda i: (0, 0), **const),
            pl.BlockSpec((d, LANES), lambda i: (0, 0), **const),
            pl.BlockSpec((GATE_COLS, d), lambda i: (0, 0), **const),
        ],
        out_specs=[
            pl.BlockSpec((tm, p), lambda i: (i, 0)),
            pl.BlockSpec((tm, LANES), lambda i: (i, 0)),
            pl.BlockSpec((GATE_COLS, tm), lambda i: (0, i)),
        ],
        out_shape=[
            jax.ShapeDtypeStruct((n, p), BF16),
            jax.ShapeDtypeStruct((n, LANES), F32),
            jax.ShapeDtypeStruct((GATE_COLS, n), F32),
        ],
        compiler_params=_params("parallel"),
        name="proj_odd",
    )(x, w, wg, wgt)


def _dilated_biases():
    def make(q_pos, k_pos, k_is_prev):
        rel = q_pos[:, None] - k_pos[None, :]
        ok = (rel >= 0) & (rel <= DSWA_SPAN)
        variants = [ok, ok & ~k_is_prev[None, :]]
        return jnp.asarray(np.stack([np.where(v, 0.0, NEG_BIG) for v in variants]), F32)

    per = DSWA_PERIOD
    kj = np.arange(2 * BLK)
    b16 = make(np.arange(BLK), kj - BLK, kj < BLK)
    qa, qi = np.divmod(np.arange(BLK), BLK // 4)
    ka, ki = np.divmod(np.arange(2 * BLK), 2 * BLK // 4)
    b4 = make(4 * qi + qa, 4 * (ki - BLK // 4) + ka, ki < BLK // 4)
    qg, qi = np.divmod(np.arange(BLK), BLK // per)
    kg, ki = np.divmod(np.arange(2 * BLK), 2 * BLK // per)
    b1 = make(per * qi + qg, per * (ki - BLK // per) + kg, ki < BLK // per)
    return b16, b4, b1


def _dilated_body(b16_ref, b4_ref, b1_ref, q_ref, kp_ref, kc_ref, vp_ref, vc_ref, y_ref,
                  kf, vf, acc, m_s, l_s, nat, q32, k32, v32):
    per = DSWA_PERIOD
    first = jnp.where(pl.program_id(1) == 0, 1, 0)
    kf[:, :BLK, :] = kp_ref[0]
    kf[:, BLK:, :] = kc_ref[0]
    vf[:, :BLK, :] = vp_ref[0]
    vf[:, BLK:, :] = vc_ref[0]
    scale = LOG2E / math.sqrt(DSWA_HEAD_DIM)

    def low_mask(rows):
        return lax.broadcasted_iota(jnp.int32, (rows, LANES), 1) < DSWA_HEAD_DIM

    def attend(units):
        scores = []
        for q2, kk, _, _ in units:
            low = low_mask(q2.shape[0])
            zero = jnp.zeros_like(q2)
            stacked = jnp.concatenate([jnp.where(low, q2, zero), jnp.where(low, zero, q2)], axis=0)
            scores.append(_dot_nt(stacked, kk))
        soft = []
        for s, (_, _, _, bias) in zip(scores, units):
            s = s + jnp.concatenate([bias, bias], axis=0)
            m = jnp.max(s, axis=-1, keepdims=True)
            p = jnp.exp2(s - m)
            soft.append((m, jnp.sum(p, axis=-1, keepdims=True), p.astype(BF16)))
        return [(m, l, _dot(p, vv)) for (m, l, p), (_, _, vv, _) in zip(soft, units)]

    def by_lane(x):
        r = x.shape[0] // 2
        return jnp.where(low_mask(r), x[:r], x[r:])

    def merge(part, m_old, l_old, acc_old):
        m, l, pv = (by_lane(x) for x in part)
        m_tot = jnp.maximum(m_old, m)
        a_old = jnp.exp2(m_old - m_tot)
        a_new = jnp.exp2(m - m_tot)
        return m_tot, l_old * a_old + l * a_new, acc_old * a_old + pv * a_new

    def gather(ref, lead, groups, rows):
        return jnp.concatenate([ref[lead + (g, rows, slice(None))] for g in groups], axis=0)

    def scatter(ref, groups, rows, value):
        n = value.shape[0] // len(groups)
        for j, g in enumerate(groups):
            ref[g, rows, :] = value[j * n:(j + 1) * n]

    state = (m_s, l_s, acc)

    def fold(parts, where):
        return [merge(part, *(gather(ref, (), groups, rows) for ref in state))
                for part, (groups, rows) in zip(parts, where)]

    def d16(it, carry):
        groups = [8 * it + j for j in range(8)]
        parts = attend([(_scaled_q(q_ref[0, g], scale), kf[g], vf[g], b16_ref[first]) for g in groups])
        for g, part in zip(groups, parts):
            for ref, x in zip(state, part):
                ref[g] = by_lane(x)
        return carry

    lax.fori_loop(0, per // 8, d16, 0)

    def d4(r4, carry):
        nq = BLK // 4
        groups = [4 * a + r4 for a in range(4)]
        units, where = [], []
        for blk in range(4):
            qrows = slice(blk * nq, (blk + 1) * nq)
            krows = slice(BLK - nq + blk * nq, BLK + (blk + 1) * nq)
            bias = b4_ref[first] if blk == 0 else b4_ref[0]
            units.append((_scaled_q(gather(q_ref, (0,), groups, qrows), scale),
                          gather(kf, (), groups, krows), gather(vf, (), groups, krows), bias))
            where.append((groups, qrows))
        for merged, (groups, rows) in zip(fold(attend(units), where), where):
            for ref, x in zip(state, merged):
                scatter(ref, groups, rows, x)
        return carry

    lax.fori_loop(0, 4, d4, 0)

    q32[...] = q_ref[0].astype(F32)
    k32[...] = kf[...].astype(F32)
    v32[...] = vf[...].astype(F32)

    def d1(it, carry):
        nq = BLK // per
        groups = list(range(per))
        units, where, blks = [], [], []
        for j in range(D1_UNITS):
            blk = D1_UNITS * it + j
            qrows = pl.ds(pl.multiple_of(blk * nq, nq), nq)
            krows = pl.ds(pl.multiple_of(BLK - nq + blk * nq, nq), 2 * nq)
            flag = jnp.where(blk == 0, first, 0)
            units.append((_scaled_q(gather(q32, (), groups, qrows), scale),
                          gather(k32, (), groups, krows).astype(BF16),
                          gather(v32, (), groups, krows).astype(BF16), b1_ref[flag]))
            where.append((groups, qrows))
            blks.append(blk)
        for j, (blk, (_, l, a)) in enumerate(zip(blks, fold(attend(units), where))):
            r = a.shape[0]
            out = a / l
            for g in groups:
                nat[j, pl.ds(g, nq, stride=per), :] = out[g * nq:(g + 1) * nq]
            y_ref[0, pl.ds(pl.multiple_of(blk * r, r), r), :] = nat[j].astype(BF16)
        return carry

    lax.fori_loop(0, DSWA_SUPER // (D1_UNITS * BLK), d1, 0)


def _dilated_attention(pq, b, s):
    per, w = DSWA_PERIOD, DSWA_W
    pairs = w // LANES
    nsb = s // DSWA_SUPER
    rows = DSWA_SUPER // per
    biases = _dilated_biases()
    blk = (1, per, rows, LANES)
    cur = lambda col: (lambda bi, sb, hp: (bi * nsb + sb, 0, 0, col * pairs + hp))
    prev = lambda col: (lambda bi, sb, hp: (bi * nsb + jnp.maximum(sb - 1, 0), 0, 0, col * pairs + hp))
    const = lambda a: pl.BlockSpec(a.shape, lambda bi, sb, hp: (0, 0, 0))
    return pl.pallas_call(
        _dilated_body,
        grid=(b, nsb, pairs),
        in_specs=[const(a) for a in biases] + [
            pl.BlockSpec(blk, cur(0)),
            pl.BlockSpec(blk, prev(1)),
            pl.BlockSpec(blk, cur(1)),
            pl.BlockSpec(blk, prev(2)),
            pl.BlockSpec(blk, cur(2)),
        ],
        out_specs=pl.BlockSpec((1, DSWA_SUPER, LANES), lambda bi, sb, hp: (bi * nsb + sb, 0, hp)),
        out_shape=jax.ShapeDtypeStruct((b * nsb, DSWA_SUPER, w), BF16),
        scratch_shapes=[
            pltpu.VMEM((per, 2 * rows, LANES), BF16),
            pltpu.VMEM((per, 2 * rows, LANES), BF16),
            pltpu.VMEM((per, rows, LANES), F32),
            pltpu.VMEM((per, rows, LANES), F32),
            pltpu.VMEM((per, rows, LANES), F32),
            pltpu.VMEM((D1_UNITS, BLK, LANES), F32),
            pltpu.VMEM((per, rows, LANES), F32),
            pltpu.VMEM((per, 2 * rows, LANES), F32),
            pltpu.VMEM((per, 2 * rows, LANES), F32),
        ],
        compiler_params=_params("parallel", "parallel", "parallel"),
        name="dilated",
    )(*biases, pq, pq, pq, pq, pq).reshape(b * s, w)


def _outproj_even_body(p_ref, halo_ref, y_ref, cw_ref, wo_ref, x_ref, g_ref, b_ref,
                       win_ref, wout_ref, g3_ref, b3_ref, o_ref, *, seq, tm, halo):
    c = CONV_CH
    pc = p_ref[...].astype(F32)
    bg, u = pc[:, :c], pc[:, c:2 * c] * pc[:, 2 * c:3 * c]
    ph = halo_ref[...].astype(F32)
    uh = ph[:, c:2 * c] * ph[:, 2 * c:3 * c]
    starts_sequence = (pl.program_id(0) * tm) % seq == 0
    uh = jnp.where(starts_sequence, 0.0, uh)
    full = jnp.concatenate([uh, u], axis=0)
    cw = cw_ref[...]
    conv = cw[2:3, :] * u + cw[1:2, :] * full[halo - 1:halo - 1 + tm] + cw[0:1, :] * full[halo - 2:halo - 2 + tm]
    y_conv = (bg * conv).astype(BF16)
    mix = _dot(y_conv, wo_ref[:c, :]) + _dot(y_ref[...], wo_ref[c:, :])
    mixed = _layer_norm(ALPHA * x_ref[...] + mix, g_ref[...], b_ref[...])
    o_ref[...] = _ffn_block(mixed, win_ref, wout_ref, g3_ref[...], b3_ref[...])


def _outproj_even(p, y_att, conv_w, w_out, x, g, b, seq, ffn):
    n, d = x.shape
    ffn_w_in, ffn_w_out, ffn_g, ffn_b = ffn
    tm = TOKEN_TILE
    halo = 16
    cw = 3 * CONV_CH
    return pl.pallas_call(
        functools.partial(_outproj_even_body, seq=seq, tm=tm, halo=halo),
        grid=(n // tm,),
        in_specs=[
            pl.BlockSpec((tm, cw), lambda i: (i, 0)),
            pl.BlockSpec((halo, cw), lambda i: (jnp.maximum(i * (tm // halo) - 1, 0), 0)),
            pl.BlockSpec((tm, DSWA_W), lambda i: (i, 0)),
            pl.BlockSpec((CONV_WIDTH, CONV_CH), lambda i: (0, 0)),
            pl.BlockSpec((d, d), lambda i: (0, 0), pipeline_mode=pl.Buffered(1)),
            pl.BlockSpec((tm, d), lambda i: (i, 0)),
            pl.BlockSpec((1, d), lambda i: (0, 0)),
            pl.BlockSpec((1, d), lambda i: (0, 0)),
        ] + _ffn_specs(d, ffn_w_out.shape[0]),
        out_specs=pl.BlockSpec((tm, d), lambda i: (i, 0)),
        out_shape=jax.ShapeDtypeStruct((n, d), F32),
        compiler_params=_params("parallel"),
        name="outproj_even_ffn",
    )(p, p, y_att, conv_w, w_out, x, g.reshape(1, d), b.reshape(1, d),
      ffn_w_in, ffn_w_out, ffn_g.reshape(1, d), ffn_b.reshape(1, d))


def _mlstm_body(*refs):
    L, dh, nh, ns = MLSTM_CHUNK, MLSTM_HEAD_DIM, MLSTM_HEADS, MLSTM_SEQS
    p_refs, gc_refs, gr_refs = refs[:ns], refs[ns:2 * ns], refs[2 * ns:3 * ns]
    bc_ref, br_ref, ng_ref, y_ref, c_scr, n_scr, m_scr = refs[3 * ns:]

    @pl.when(pl.program_id(1) == 0)
    def _():
        c_scr[...] = jnp.zeros_like(c_scr)
        n_scr[...] = jnp.zeros_like(n_scr)
        m_scr[...] = jnp.zeros_like(m_scr)

    row = lax.broadcasted_iota(jnp.int32, (L, L), 0)
    col = lax.broadcasted_iota(jnp.int32, (L, L), 1)
    causal = col <= row
    tri = causal.astype(BF16)
    tri_t = (row <= col).astype(BF16)

    scale = 1.0 / math.sqrt(dh)
    rep = lambda a: jnp.broadcast_to(a, (L, LANES))

    chains = [(j, h) for j in range(ns) for h in range(nh)]
    ids = range(len(chains))

    def part(i, w):
        j, h = chains[i]
        return p_refs[j][:, w * MLSTM_W + h * dh:w * MLSTM_W + (h + 1) * dh]

    q, k, v = ([part(i, w) for i in ids] for w in range(3))
    gcol = [gc_refs[j][...] + bc_ref[...] for j in range(ns)]
    grow = [gr_refs[j][...] + br_ref[...] for j in range(ns)]
    bcum_c = [_split_dot(tri, _neg_softplus(-g)) for g in gcol]
    bcum_r = [_split_dot_r(_neg_softplus(-g), tri_t) for g in grow]
    i_c = [rep(gcol[j][:, h:h + 1]) for j, h in chains]
    b_c = [rep(bcum_c[j][:, nh + h:nh + h + 1]) for j, h in chains]
    i_r = [grow[j][h:h + 1, :] for j, h in chains]
    b_r = [bcum_r[j][nh + h:nh + h + 1, :] for j, h in chains]
    m_prev = [m_scr[i] for i in ids]
    c_prev = [c_scr[i] for i in ids]
    n_prev = [n_scr[i] for i in ids]

    s = [_dot_nt(q[i], k[i]) for i in ids]
    q_c = [_dot(q[i], c_prev[i].astype(BF16)) for i in ids]

    for i in ids:
        b_tot = b_c[i][L - 1:L, :]
        d_state = b_tot - b_c[i] + i_c[i]
        m_new = jnp.maximum(b_tot + m_prev[i], jnp.max(d_state, axis=0, keepdims=True))
        kw = k[i].astype(F32) * (jnp.exp(d_state - m_new) * scale)
        decay = jnp.exp(b_tot + m_prev[i] - m_new)
        c_scr[i] = decay * c_prev[i] + _dot_tn(kw.astype(BF16), v[i])
        n_scr[i] = decay * n_prev[i] + jnp.sum(kw, axis=0, keepdims=True)
        m_scr[i] = m_new

    w_inter, m_t, qk = [], [], []
    for i in ids:
        d_intra = jnp.where(causal, b_c[i] - b_r[i] + i_r[i], NEG_BIG)
        d_inter = b_c[i] + m_prev[i]
        m_t.append(jnp.maximum(d_inter, jnp.max(d_intra, axis=-1, keepdims=True)))
        w_inter.append(jnp.exp(d_inter - m_t[i]))
        qk.append(s[i] * scale * jnp.exp(d_intra - m_t[i]))
    qk_v = [_dot(qk[i].astype(BF16), v[i]) for i in ids]

    for i, (j, h) in enumerate(chains):
        cols = slice(h * dh, (h + 1) * dh)
        num = w_inter[i] * q_c[i] + qk_v[i]
        den = (w_inter[i] * jnp.sum(q[i].astype(F32) * n_prev[i], axis=-1, keepdims=True)
               + jnp.sum(qk[i], axis=-1, keepdims=True))
        h_tilde = num / jnp.maximum(jnp.abs(den), jnp.exp(-m_t[i]))
        cell = jax.nn.sigmoid(part(i, 3).astype(F32)) * h_tilde
        mu = jnp.mean(cell, axis=-1, keepdims=True)
        cc = cell - mu
        var = jnp.mean(cc * cc, axis=-1, keepdims=True)
        y_ref[0, j, :, cols] = (cc * lax.rsqrt(var + LN_EPS) * ng_ref[:, cols]).astype(BF16)


def _mlstm(p, gcol, grow, b_i, b_f, norm_g, b, s):
    L, ns = MLSTM_CHUNK, MLSTM_SEQS
    nc = s // L
    chains = ns * MLSTM_HEADS
    bias = jnp.concatenate([b_i, b_f]).astype(F32)
    bias_c = jnp.zeros((1, LANES), F32).at[0, :GATE_COLS].set(bias)
    bias_r = bias.reshape(GATE_COLS, 1)
    tok = lambda j: (lambda bi, c: ((bi * ns + j) * nc + c, 0))
    tok_t = lambda j: (lambda bi, c: (0, (bi * ns + j) * nc + c))
    const = lambda bi, c: (0, 0)
    return pl.pallas_call(
        _mlstm_body,
        grid=(b // ns, nc),
        in_specs=(
            [pl.BlockSpec((L, 4 * MLSTM_W), tok(j)) for j in range(ns)]
            + [pl.BlockSpec((L, LANES), tok(j)) for j in range(ns)]
            + [pl.BlockSpec((GATE_COLS, L), tok_t(j)) for j in range(ns)]
            + [pl.BlockSpec((1, LANES), const), pl.BlockSpec((GATE_COLS, 1), const),
               pl.BlockSpec((1, MLSTM_W), const)]),
        out_specs=pl.BlockSpec((1, ns, L, MLSTM_W), lambda bi, c: (bi, 0, c, 0)),
        out_shape=jax.ShapeDtypeStruct((b // ns, ns, s, MLSTM_W), BF16),
        scratch_shapes=[
            pltpu.VMEM((chains, MLSTM_HEAD_DIM, MLSTM_HEAD_DIM), F32),
            pltpu.VMEM((chains, 1, MLSTM_HEAD_DIM), F32),
            pltpu.VMEM((chains, 1, LANES), F32),
        ],
        compiler_params=_params("parallel", "arbitrary"),
        name="mlstm",
    )(*([p] * ns + [gcol] * ns + [grow] * ns), bias_c, bias_r,
      norm_g.reshape(1, MLSTM_W)).reshape(b * s, MLSTM_W)


def _stick_body(bias_ref, q_ref, k_ref, v_ref, y_ref, acc_scr, c_scr):
    t, w = SB_TILE, SB_WINDOW
    units = range(SB_UNITS)
    starts = [(pl.program_id(2) * SB_UNITS + u) * t for u in units]

    def suffix(n):
        return (lax.broadcasted_iota(jnp.int32, (n, n), 0)
                >= lax.broadcasted_iota(jnp.int32, (n, n), 1)).astype(BF16)

    suffix_t, suffix_w = suffix(t), suffix(w)
    low = lax.broadcasted_iota(jnp.int32, (t, LANES), 1) < SB_HEAD_DIM

    def stacked(q2):
        zero = jnp.zeros_like(q2)
        return jnp.concatenate([jnp.where(low, q2, zero), jnp.where(low, zero, q2)], axis=0)

    def keys(first, n):
        return pl.ds(pl.multiple_of(first, t), n)

    def weights(z, incl, c_later):
        return jnp.exp2(z + incl + c_later).astype(BF16)

    def log_mass(lg):
        return jnp.sum(lg, axis=-1, keepdims=True)

    qs = [stacked(_scaled_q(q_ref[0, u * t:(u + 1) * t, :], LOG2E / math.sqrt(SB_HEAD_DIM))) for u in units]
    win = [jnp.maximum(s0 - w, 0) for s0 in starts]
    causal = bias_ref[...]
    causal = jnp.concatenate([causal, causal], axis=0)
    col_w = lax.broadcasted_iota(jnp.int32, (1, w), 1)
    earlier = [jnp.where(col_w < starts[u] - win[u], 0.0, NEG_BIG) for u in units]
    z_diag = [_dot_nt(qs[u], k_ref[0, keys(starts[u], t), :]) + causal for u in units]
    z_prev = [_dot_nt(qs[u], k_ref[0, keys(win[u], w), :]) + earlier[u] for u in units]
    lg_diag = [_neg_softplus2(z) for z in z_diag]
    lg_prev = [_neg_softplus2(z) for z in z_prev]
    in_diag = [_split_dot_r(lg, suffix_t) for lg in lg_diag]
    in_prev = [_split_dot_r(lg, suffix_w) for lg in lg_prev]
    c_diag = [log_mass(lg) for lg in lg_diag]
    a_diag = [weights(z_diag[u], in_diag[u], 0.0) for u in units]
    a_prev = [weights(z_prev[u], in_prev[u], c_diag[u]) for u in units]
    c_both = [c_diag[u] + log_mass(lg_prev[u]) for u in units]
    for u in units:
        acc_scr[u] = (_dot(a_diag[u], v_ref[0, keys(starts[u], t), :])
                      + _dot(a_prev[u], v_ref[0, keys(win[u], w), :]))
        c_scr[u] = jnp.broadcast_to(c_both[u], c_scr.shape[1:])

    def cond(carry):
        j, c_max = carry
        return (j >= 0) & (c_max > SB_LOG2_FLOOR)

    for u in units:
        def body(carry, u=u):
            j, _ = carry
            c_old = c_scr[u]
            z = _dot_nt(qs[u], k_ref[0, keys(j * t, t), :])
            lg = _neg_softplus2(z)
            a = weights(z, _split_dot_r(lg, suffix_t), jnp.concatenate([c_old] * (t // LANES), axis=1))
            acc_scr[u] += _dot(a, v_ref[0, keys(j * t, t), :])
            c_new = c_old + log_mass(lg)
            c_scr[u] = c_new
            return j - 1, jnp.max(c_new)

        lax.while_loop(cond, body, (win[u] // t - 1, jnp.max(c_both[u])))
        out = acc_scr[u]
        y_ref[0, u * t:(u + 1) * t, :] = jnp.where(low, out[:t], out[t:]).astype(BF16)


def _stick_breaking(p, b, s, col0):
    t = SB_TILE
    step = SB_UNITS * t
    pairs = SB_W // LANES
    pv = p.reshape(b, s, p.shape[1])
    c0 = col0 // LANES
    idx = np.arange(t)
    causal = jnp.asarray(np.where(idx[None, :] < idx[:, None], 0.0, NEG_BIG), F32)
    return pl.pallas_call(
        _stick_body,
        grid=(b, pairs, s // step),
        in_specs=[
            pl.BlockSpec((t, t), lambda bi, hp, qi: (0, 0)),
            pl.BlockSpec((1, step, LANES), lambda bi, hp, qi: (bi, qi, c0 + hp)),
            pl.BlockSpec((1, s, LANES), lambda bi, hp, qi: (bi, 0, c0 + pairs + hp)),
            pl.BlockSpec((1, s, LANES), lambda bi, hp, qi: (bi, 0, c0 + 2 * pairs + hp)),
        ],
        out_specs=pl.BlockSpec((1, step, LANES), lambda bi, hp, qi: (bi, qi, hp)),
        out_shape=jax.ShapeDtypeStruct((b, s, SB_W), BF16),
        scratch_shapes=[pltpu.VMEM((SB_UNITS, 2 * t, LANES), F32), pltpu.VMEM((SB_UNITS, 2 * t, LANES), F32)],
        compiler_params=_params("parallel", "parallel", "arbitrary"),
        name="stick_breaking",
    )(causal, pv, pv, pv).reshape(b * s, SB_W)


def _outproj_odd_body(ym_ref, ys_ref, wo_ref, x_ref, g_ref, b_ref, win_ref, wout_ref, g3_ref, b3_ref, o_ref):
    mix = _dot(ym_ref[...], wo_ref[:MLSTM_W, :]) + _dot(ys_ref[...], wo_ref[MLSTM_W:, :])
    mixed = _layer_norm(ALPHA * x_ref[...] + mix, g_ref[...], b_ref[...])
    o_ref[...] = _ffn_block(mixed, win_ref, wout_ref, g3_ref[...], b3_ref[...])


def _outproj_odd(y_m, y_s, w_out, x, g, b, ffn):
    n, d = x.shape
    ffn_w_in, ffn_w_out, ffn_g, ffn_b = ffn
    tm = TOKEN_TILE
    return pl.pallas_call(
        _outproj_odd_body,
        grid=(n // tm,),
        in_specs=[
            pl.BlockSpec((tm, MLSTM_W), lambda i: (i, 0)),
            pl.BlockSpec((tm, SB_W), lambda i: (i, 0)),
            pl.BlockSpec((d, d), lambda i: (0, 0), pipeline_mode=pl.Buffered(1)),
            pl.BlockSpec((tm, d), lambda i: (i, 0)),
            pl.BlockSpec((1, d), lambda i: (0, 0)),
            pl.BlockSpec((1, d), lambda i: (0, 0)),
        ] + _ffn_specs(d, ffn_w_out.shape[0]),
        out_specs=pl.BlockSpec((tm, d), lambda i: (i, 0)),
        out_shape=jax.ShapeDtypeStruct((n, d), F32),
        compiler_params=_params("parallel"),
        name="outproj_odd_ffn",
    )(y_m, y_s, w_out, x, g.reshape(1, d), b.reshape(1, d),
      ffn_w_in, ffn_w_out, ffn_g.reshape(1, d), ffn_b.reshape(1, d))


def _even_mixer(x, w_in, conv_w, w_out, g, b, batch, seq, ffn):
    pc, pq = _proj_even(x, w_in.astype(BF16))
    y_att = _dilated_attention(pq, batch, seq)
    return _outproj_even(pc, y_att, conv_w, w_out.astype(BF16), x, g, b, seq, ffn)


def _odd_mixer(x, w_in, b_i, b_f, norm_g, w_out, g, b, batch, seq, ffn):
    gate0 = 4 * MLSTM_W
    w_main = jnp.concatenate([w_in[:, :gate0], w_in[:, gate0 + GATE_COLS:]], axis=1).astype(BF16)
    w_gate = w_in[:, gate0:gate0 + GATE_COLS].astype(BF16)
    w_gate_pad = jnp.pad(w_gate, ((0, 0), (0, LANES - GATE_COLS)))
    p, gcol, grow = _proj_gates(x, w_main, w_gate_pad, w_gate.T)
    y_m = _mlstm(p, gcol, grow, b_i, b_f, norm_g, batch, seq)
    y_s = _stick_breaking(p, batch, seq, gate0)
    return _outproj_odd(y_m, y_s, w_out.astype(BF16), x, g, b, ffn)


def kernel(x, l0_ffn1_w_in, l0_ffn1_w_out, l0_ln1_g, l0_ln1_b, l0_mix_w_in, l0_conv_w, l0_mix_w_out, l0_ln2_g, l0_ln2_b, l0_ffn2_w_in, l0_ffn2_w_out, l0_ln3_g, l0_ln3_b, l1_ffn1_w_in, l1_ffn1_w_out, l1_ln1_g, l1_ln1_b, l1_mix_w_in, l1_mlstm_b_i, l1_mlstm_b_f, l1_mlstm_norm_g, l1_mix_w_out, l1_ln2_g, l1_ln2_b, l1_ffn2_w_in, l1_ffn2_w_out, l1_ln3_g, l1_ln3_b):
    batch, seq, d = x.shape
    ffn = lambda w_in, w_out, g, b: (w_in.astype(BF16), w_out.astype(BF16), g, b)
    t = x.reshape(batch * seq, d)
    t = _ffn_ln(t, ffn(l0_ffn1_w_in, l0_ffn1_w_out, l0_ln1_g, l0_ln1_b))
    t = _even_mixer(t, l0_mix_w_in, l0_conv_w, l0_mix_w_out, l0_ln2_g, l0_ln2_b, batch, seq,
                    ffn(l0_ffn2_w_in, l0_ffn2_w_out, l0_ln3_g, l0_ln3_b))
    t = _ffn_ln(t, ffn(l1_ffn1_w_in, l1_ffn1_w_out, l1_ln1_g, l1_ln1_b))
    t = _odd_mixer(t, l1_mix_w_in, l1_mlstm_b_i, l1_mlstm_b_f, l1_mlstm_norm_g, l1_mix_w_out,
                   l1_ln2_g, l1_ln2_b, batch, seq,
                   ffn(l1_ffn2_w_in, l1_ffn2_w_out, l1_ln3_g, l1_ln3_b))
    return t.reshape(batch, seq, d)
```

```python
import functools
import math

import jax
import jax.numpy as jnp
import numpy as np
from jax import lax
from jax.experimental import pallas as pl
from jax.experimental.pallas import tpu as pltpu

F32 = jnp.float32
BF16 = jnp.bfloat16

DEPTH = 2
ALPHA = (2 * DEPTH) ** 0.25
FFN_RES_WEIGHT = 0.5
LN_EPS = 1e-5

LANES = 128
MXU_TILE = 256
BLK = 128
CONV_CH = 256
CONV_WIDTH = 3
DSWA_HEAD_DIM = 64
DSWA_W = 768
DSWA_SPAN = 128
DSWA_PERIOD = 16
DSWA_SUPER = DSWA_PERIOD * BLK
D1_UNITS = 4
MLSTM_HEADS = 4
MLSTM_W = 512
MLSTM_HEAD_DIM = 128
MLSTM_CHUNK = 128
MLSTM_SEQS = 1
SB_HEAD_DIM = 64
SB_W = 512
GATE_COLS = 2 * MLSTM_HEADS

TOKEN_TILE = 512
FFN_TOKEN_TILE = 1024
SB_TILE = 128
SB_WINDOW = 256
SB_UNITS = 4
SB_SPENT_LOG2 = 127.0
LOG2E = math.log2(math.e)
NEG_BIG = -1e30
VMEM_LIMIT = 56 * 1024 * 1024


def _params(*sem):
    return pltpu.CompilerParams(dimension_semantics=sem, vmem_limit_bytes=VMEM_LIMIT)


def _layer_norm(z, g, b):
    mu = jnp.mean(z, axis=-1, keepdims=True)
    zc = z - mu
    var = jnp.mean(zc * zc, axis=-1, keepdims=True)
    return zc * lax.rsqrt(var + LN_EPS) * g + b


def _dot(a, b):
    return jnp.dot(a, b, preferred_element_type=F32)


def _dot_nt(a, b):
    return lax.dot_general(a, b, (((1,), (1,)), ((), ())), preferred_element_type=F32)


def _dot_tn(a, b):
    return lax.dot_general(a, b, (((0,), (0,)), ((), ())), preferred_element_type=F32)


def _split_dot(tri, x):
    hi = x.astype(BF16)
    lo = (x - hi.astype(F32)).astype(BF16)
    return _dot(jnp.concatenate([tri, tri], axis=1), jnp.concatenate([hi, lo], axis=0))


def _split_dot_r(x, tri):
    hi = x.astype(BF16)
    lo = (x - hi.astype(F32)).astype(BF16)
    return _dot(jnp.concatenate([hi, lo], axis=1), jnp.concatenate([tri, tri], axis=0))


def _neg_softplus(z):
    return -(jnp.maximum(z, 0.0) + jnp.log(1.0 + jnp.exp(-jnp.abs(z))))


def _softplus2(z2):
    return jnp.maximum(z2, 0.0) + jnp.log(1.0 + jnp.exp2(-jnp.abs(z2))) * LOG2E


def _scaled_q(q, scale):
    return (q.astype(F32) * scale).astype(BF16)


def _ffn_chunks(d_ff):
    tiles = d_ff // MXU_TILE
    cut = (tiles + 1) // 2 * MXU_TILE
    return ((0, cut), (cut, d_ff))


def _ffn_block(x, win_ref, wout_ref, g, b):
    d_ff = wout_ref.shape[0]
    halves = range(2)
    r = x.shape[0] // 2
    xs = [x[h * r:(h + 1) * r] for h in halves]
    xb = [t.astype(BF16) for t in xs]
    acc = [None, None]
    for lo, hi in _ffn_chunks(d_ff):
        gate_up = [(_dot(xb[h], win_ref[:, lo:hi]), _dot(xb[h], win_ref[:, d_ff + lo:d_ff + hi])) for h in halves]
        for h in halves:
            gate, up = gate_up[h]
            act = (gate * jax.nn.sigmoid(gate) * up).astype(BF16)
            part = _dot(act, wout_ref[lo:hi, :])
            acc[h] = part if acc[h] is None else acc[h] + part
    return jnp.concatenate([_layer_norm(ALPHA * xs[h] + FFN_RES_WEIGHT * acc[h], g, b) for h in halves], axis=0)


def _ffn_specs(d, d_ff):
    const = dict(pipeline_mode=pl.Buffered(1))
    return [
        pl.BlockSpec((d, 2 * d_ff), lambda i: (0, 0), **const),
        pl.BlockSpec((d_ff, d), lambda i: (0, 0), **const),
        pl.BlockSpec((1, d), lambda i: (0, 0)),
        pl.BlockSpec((1, d), lambda i: (0, 0)),
    ]


def _ffn_ln_body(x_ref, win_ref, wout_ref, g_ref, b_ref, o_ref):
    o_ref[...] = _ffn_block(x_ref[...], win_ref, wout_ref, g_ref[...], b_ref[...])


def _ffn_ln(x, ffn):
    n, d = x.shape
    w_in, w_out, g, b = ffn
    d_ff = w_out.shape[0]
    tm = FFN_TOKEN_TILE
    return pl.pallas_call(
        _ffn_ln_body,
        grid=(n // tm,),
        in_specs=[pl.BlockSpec((tm, d), lambda i: (i, 0))] + _ffn_specs(d, d_ff),
        out_specs=pl.BlockSpec((tm, d), lambda i: (i, 0)),
        out_shape=jax.ShapeDtypeStruct((n, d), F32),
        compiler_params=_params("parallel"),
        name="ffn_ln",
    )(x, w_in, w_out, g.reshape(1, d), b.reshape(1, d))


def _proj_even_body(x_ref, perm_ref, w_ref, pc_ref, pq_ref, *, tm, tn):
    cw = 3 * CONV_CH
    xb = x_ref[...].astype(BF16)
    pc_ref[...] = _dot(xb, w_ref[:, :cw]).astype(BF16)
    xg = _dot(perm_ref[...], xb).astype(BF16)
    rows = tm // DSWA_PERIOD
    for c in range((w_ref.shape[1] - cw) // tn):
        res = _dot(xg, w_ref[:, cw + c * tn:cw + (c + 1) * tn]).astype(BF16)
        pq_ref[0, :, :, c * tn:(c + 1) * tn] = res.reshape(DSWA_PERIOD, rows, tn)


def _proj_even(x, w):
    n, d = x.shape
    cw = 3 * CONV_CH
    qw = w.shape[1] - cw
    tm = TOKEN_TILE
    tiles = DSWA_SUPER // tm
    rows = tm // DSWA_PERIOD
    dst = np.arange(tm)
    perm = np.zeros((tm, tm), np.float32)
    perm[dst, (dst % rows) * DSWA_PERIOD + dst // rows] = 1.0
    const = dict(pipeline_mode=pl.Buffered(1))
    return pl.pallas_call(
        functools.partial(_proj_even_body, tm=tm, tn=768),
        grid=(n // tm,),
        in_specs=[
            pl.BlockSpec((tm, d), lambda i: (i, 0)),
            pl.BlockSpec((tm, tm), lambda i: (0, 0), **const),
            pl.BlockSpec((d, cw + qw), lambda i: (0, 0), **const),
        ],
        out_specs=[
            pl.BlockSpec((tm, cw), lambda i: (i, 0)),
            pl.BlockSpec((1, DSWA_PERIOD, rows, qw), lambda i: (i // tiles, 0, i % tiles, 0)),
        ],
        out_shape=[
            jax.ShapeDtypeStruct((n, cw), BF16),
            jax.ShapeDtypeStruct((n // DSWA_SUPER, DSWA_PERIOD, DSWA_SUPER // DSWA_PERIOD, qw), BF16),
        ],
        compiler_params=_params("parallel"),
        name="proj_even",
    )(x, jnp.asarray(perm, BF16), w)


def _proj_gates_body(x_ref, w_ref, wg_ref, wgt_ref, o_ref, gc_ref, gr_ref, *, tn):
    xb = x_ref[...].astype(BF16)
    for c in range(w_ref.shape[1] // tn):
        o_ref[:, c * tn:(c + 1) * tn] = _dot(xb, w_ref[:, c * tn:(c + 1) * tn]).astype(BF16)
    gc_ref[...] = _dot(xb, wg_ref[...])
    gr_ref[...] = _dot_nt(wgt_ref[...], xb)


def _proj_gates(x, w, wg, wgt):
    n, d = x.shape
    p = w.shape[1]
    tm = TOKEN_TILE
    const = dict(pipeline_mode=pl.Buffered(1))
    return pl.pallas_call(
        functools.partial(_proj_gates_body, tn=512),
        grid=(n // tm,),
        in_specs=[
            pl.BlockSpec((tm, d), lambda i: (i, 0)),
            pl.BlockSpec((d, p), lambda i: (0, 0), **const),
            pl.BlockSpec((d, LANES), lambda i: (0, 0), **const),
            pl.BlockSpec((GATE_COLS, d), lambda i: (0, 0), **const),
        ],
        out_specs=[
            pl.BlockSpec((tm, p), lambda i: (i, 0)),
            pl.BlockSpec((tm, LANES), lambda i: (i, 0)),
            pl.BlockSpec((GATE_COLS, tm), lambda i: (0, i)),
        ],
        out_shape=[
            jax.ShapeDtypeStruct((n, p), BF16),
            jax.ShapeDtypeStruct((n, LANES), F32),
            jax.ShapeDtypeStruct((GATE_COLS, n), F32),
        ],
        compiler_params=_params("parallel"),
        name="proj_odd",
    )(x, w, wg, wgt)


def _dilated_biases():
    def make(q_pos, k_pos, k_is_prev):
        rel = q_pos[:, None] - k_pos[None, :]
        ok = (rel >= 0) & (rel <= DSWA_SPAN)
        variants = [ok, ok & ~k_is_prev[None, :]]
        return jnp.asarray(np.stack([np.where(v, 0.0, NEG_BIG) for v in variants]), F32)

    per = DSWA_PERIOD
    kj = np.arange(2 * BLK)
    b16 = make(np.arange(BLK), kj - BLK, kj < BLK)
    qa, qi = np.divmod(np.arange(BLK), BLK // 4)
    ka, ki = np.divmod(np.arange(2 * BLK), 2 * BLK // 4)
    b4 = make(4 * qi + qa, 4 * (ki - BLK // 4) + ka, ki < BLK // 4)
    qg, qi = np.divmod(np.arange(BLK), BLK // per)
    kg, ki = np.divmod(np.arange(2 * BLK), 2 * BLK // per)
    b1 = make(per * qi + qg, per * (ki - BLK // per) + kg, ki < BLK // per)
    return b16, b4, b1


def _dilated_body(b16_ref, b4_ref, b1_ref, q_ref, kp_ref, kc_ref, vp_ref, vc_ref, y_ref,
                  kf, vf, acc, m_s, l_s, nat, q32, k32, v32):
    per = DSWA_PERIOD
    first = jnp.where(pl.program_id(1) == 0, 1, 0)
    kf[:, :BLK, :] = kp_ref[0]
    kf[:, BLK:, :] = kc_ref[0]
    vf[:, :BLK, :] = vp_ref[0]
    vf[:, BLK:, :] = vc_ref[0]
    scale = LOG2E / math.sqrt(DSWA_HEAD_DIM)

    def low_mask(rows):
        return lax.broadcasted_iota(jnp.int32, (rows, LANES), 1) < DSWA_HEAD_DIM

    def attend(units):
        scores = []
        for q2, kk, _, _ in units:
            low = low_mask(q2.shape[0])
            zero = jnp.zeros_like(q2)
            stacked = jnp.concatenate([jnp.where(low, q2, zero), jnp.where(low, zero, q2)], axis=0)
            scores.append(_dot_nt(stacked, kk))
        soft = []
        for s, (_, _, _, bias) in zip(scores, units):
            s = s + jnp.concatenate([bias, bias], axis=0)
            m = jnp.max(s, axis=-1, keepdims=True)
            p = jnp.exp2(s - m)
            soft.append((m, jnp.sum(p, axis=-1, keepdims=True), p.astype(BF16)))
        return [(m, l, _dot(p, vv)) for (m, l, p), (_, _, vv, _) in zip(soft, units)]

    def by_lane(x):
        r = x.shape[0] // 2
        return jnp.where(low_mask(r), x[:r], x[r:])

    def merge(part, m_old, l_old, acc_old):
        m, l, pv = (by_lane(x) for x in part)
        m_tot = jnp.maximum(m_old, m)
        a_old = jnp.exp2(m_old - m_tot)
        a_new = jnp.exp2(m - m_tot)
        return m_tot, l_old * a_old + l * a_new, acc_old * a_old + pv * a_new

    def gather(ref, lead, groups, rows):
        return jnp.concatenate([ref[lead + (g, rows, slice(None))] for g in groups], axis=0)

    def scatter(ref, groups, rows, value):
        n = value.shape[0] // len(groups)
        for j, g in enumerate(groups):
            ref[g, rows, :] = value[j * n:(j + 1) * n]

    state = (m_s, l_s, acc)

    def fold(parts, where):
        return [merge(part, *(gather(ref, (), groups, rows) for ref in state))
                for part, (groups, rows) in zip(parts, where)]

    def d16(it, carry):
        groups = [8 * it + j for j in range(8)]
        parts = attend([(_scaled_q(q_ref[0, g], scale), kf[g], vf[g], b16_ref[first]) for g in groups])
        for g, part in zip(groups, parts):
            for ref, x in zip(state, part):
                ref[g] = by_lane(x)
        return carry

    lax.fori_loop(0, per // 8, d16, 0)

    def d4(r4, carry):
        nq = BLK // 4
        groups = [4 * a + r4 for a in range(4)]
        units, where = [], []
        for blk in range(4):
            qrows = slice(blk * nq, (blk + 1) * nq)
            krows = slice(BLK - nq + blk * nq, BLK + (blk + 1) * nq)
            bias = b4_ref[first] if blk == 0 else b4_ref[0]
            units.append((_scaled_q(gather(q_ref, (0,), groups, qrows), scale),
                          gather(kf, (), groups, krows), gather(vf, (), groups, krows), bias))
            where.append((groups, qrows))
        for merged, (groups, rows) in zip(fold(attend(units), where), where):
            for ref, x in zip(state, merged):
                scatter(ref, groups, rows, x)
        return carry

    lax.fori_loop(0, 4, d4, 0)

    q32[...] = q_ref[0].astype(F32)
    k32[:, :BLK, :] = kp_ref[0].astype(F32)
    k32[:, BLK:, :] = kc_ref[0].astype(F32)
    v32[:, :BLK, :] = vp_ref[0].astype(F32)
    v32[:, BLK:, :] = vc_ref[0].astype(F32)

    def d1(it, carry):
        nq = BLK // per
        groups = list(range(per))
        units, where, blks = [], [], []
        for j in range(D1_UNITS):
            blk = D1_UNITS * it + j
            qrows = pl.ds(pl.multiple_of(blk * nq, nq), nq)
            krows = pl.ds(pl.multiple_of(BLK - nq + blk * nq, nq), 2 * nq)
            flag = jnp.where(blk == 0, first, 0)
            units.append((_scaled_q(gather(q32, (), groups, qrows), scale),
                          gather(k32, (), groups, krows).astype(BF16),
                          gather(v32, (), groups, krows).astype(BF16), b1_ref[flag]))
            where.append((groups, qrows))
            blks.append(blk)
        for j, (blk, (_, l, a)) in enumerate(zip(blks, fold(attend(units), where))):
            r = a.shape[0]
            out = a / l
            for g in groups:
                nat[j, pl.ds(g, nq, stride=per), :] = out[g * nq:(g + 1) * nq]
            y_ref[0, pl.ds(pl.multiple_of(blk * r, r), r), :] = nat[j].astype(BF16)
        return carry

    lax.fori_loop(0, DSWA_SUPER // (D1_UNITS * BLK), d1, 0)


def _dilated_attention(pq, b, s):
    per, w = DSWA_PERIOD, DSWA_W
    pairs = w // LANES
    nsb = s // DSWA_SUPER
    rows = DSWA_SUPER // per
    biases = _dilated_biases()
    blk = (1, per, rows, LANES)
    cur = lambda col: (lambda bi, sb, hp: (bi * nsb + sb, 0, 0, col * pairs + hp))
    prev = lambda col: (lambda bi, sb, hp: (bi * nsb + jnp.maximum(sb - 1, 0), 0, 0, col * pairs + hp))
    const = lambda a: pl.BlockSpec(a.shape, lambda bi, sb, hp: (0, 0, 0))
    return pl.pallas_call(
        _dilated_body,
        grid=(b, nsb, pairs),
        in_specs=[const(a) for a in biases] + [
            pl.BlockSpec(blk, cur(0)),
            pl.BlockSpec(blk, prev(1)),
            pl.BlockSpec(blk, cur(1)),
            pl.BlockSpec(blk, prev(2)),
            pl.BlockSpec(blk, cur(2)),
        ],
        out_specs=pl.BlockSpec((1, DSWA_SUPER, LANES), lambda bi, sb, hp: (bi * nsb + sb, 0, hp)),
        out_shape=jax.ShapeDtypeStruct((b * nsb, DSWA_SUPER, w), BF16),
        scratch_shapes=[
            pltpu.VMEM((per, 2 * rows, LANES), BF16),
            pltpu.VMEM((per, 2 * rows, LANES), BF16),
            pltpu.VMEM((per, rows, LANES), F32),
            pltpu.VMEM((per, rows, LANES), F32),
            pltpu.VMEM((per, rows, LANES), F32),
            pltpu.VMEM((D1_UNITS, BLK, LANES), F32),
            pltpu.VMEM((per, rows, LANES), F32),
            pltpu.VMEM((per, 2 * rows, LANES), F32),
            pltpu.VMEM((per, 2 * rows, LANES), F32),
        ],
        compiler_params=_params("parallel", "parallel", "parallel"),
        name="dilated",
    )(*biases, pq, pq, pq, pq, pq).reshape(b * s, w)


def _outproj_even_body(p_ref, halo_ref, y_ref, cw_ref, wo_ref, x_ref, g_ref, b_ref,
                       win_ref, wout_ref, g3_ref, b3_ref, o_ref, *, seq, tm, halo):
    c = CONV_CH
    pc = p_ref[...].astype(F32)
    bg, u = pc[:, :c], pc[:, c:2 * c] * pc[:, 2 * c:3 * c]
    ph = halo_ref[...].astype(F32)
    uh = ph[:, c:2 * c] * ph[:, 2 * c:3 * c]
    starts_sequence = (pl.program_id(0) * tm) % seq == 0
    uh = jnp.where(starts_sequence, 0.0, uh)
    full = jnp.concatenate([uh, u], axis=0)
    cw = cw_ref[...]
    conv = cw[2:3, :] * u + cw[1:2, :] * full[halo - 1:halo - 1 + tm] + cw[0:1, :] * full[halo - 2:halo - 2 + tm]
    y_conv = (bg * conv).astype(BF16)
    mix = _dot(y_conv, wo_ref[:c, :]) + _dot(y_ref[...], wo_ref[c:, :])
    mixed = _layer_norm(ALPHA * x_ref[...] + mix, g_ref[...], b_ref[...])
    o_ref[...] = _ffn_block(mixed, win_ref, wout_ref, g3_ref[...], b3_ref[...])


def _outproj_even(p, y_att, conv_w, w_out, x, g, b, seq, ffn):
    n, d = x.shape
    ffn_w_in, ffn_w_out, ffn_g, ffn_b = ffn
    tm = TOKEN_TILE
    halo = 16
    cw = 3 * CONV_CH
    return pl.pallas_call(
        functools.partial(_outproj_even_body, seq=seq, tm=tm, halo=halo),
        grid=(n // tm,),
        in_specs=[
            pl.BlockSpec((tm, cw), lambda i: (i, 0)),
            pl.BlockSpec((halo, cw), lambda i: (jnp.maximum(i * (tm // halo) - 1, 0), 0)),
            pl.BlockSpec((tm, DSWA_W), lambda i: (i, 0)),
            pl.BlockSpec((CONV_WIDTH, CONV_CH), lambda i: (0, 0)),
            pl.BlockSpec((d, d), lambda i: (0, 0), pipeline_mode=pl.Buffered(1)),
            pl.BlockSpec((tm, d), lambda i: (i, 0)),
            pl.BlockSpec((1, d), lambda i: (0, 0)),
            pl.BlockSpec((1, d), lambda i: (0, 0)),
        ] + _ffn_specs(d, ffn_w_out.shape[0]),
        out_specs=pl.BlockSpec((tm, d), lambda i: (i, 0)),
        out_shape=jax.ShapeDtypeStruct((n, d), F32),
        compiler_params=_params("parallel"),
        name="outproj_even_ffn",
    )(p, p, y_att, conv_w, w_out, x, g.reshape(1, d), b.reshape(1, d),
      ffn_w_in, ffn_w_out, ffn_g.reshape(1, d), ffn_b.reshape(1, d))


def _mlstm_body(*refs):
    L, dh, nh, ns = MLSTM_CHUNK, MLSTM_HEAD_DIM, MLSTM_HEADS, MLSTM_SEQS
    p_refs, gc_refs, gr_refs = refs[:ns], refs[ns:2 * ns], refs[2 * ns:3 * ns]
    bc_ref, br_ref, ng_ref, y_ref, c_scr, n_scr, m_scr = refs[3 * ns:]

    @pl.when(pl.program_id(1) == 0)
    def _():
        c_scr[...] = jnp.zeros_like(c_scr)
        n_scr[...] = jnp.zeros_like(n_scr)
        m_scr[...] = jnp.zeros_like(m_scr)

    row = lax.broadcasted_iota(jnp.int32, (L, L), 0)
    col = lax.broadcasted_iota(jnp.int32, (L, L), 1)
    causal = col <= row
    tri = causal.astype(BF16)
    tri_t = (row <= col).astype(BF16)

    scale = 1.0 / math.sqrt(dh)
    rep = lambda a: jnp.broadcast_to(a, (L, LANES))

    chains = [(j, h) for j in range(ns) for h in range(nh)]
    ids = range(len(chains))

    def part(i, w):
        j, h = chains[i]
        return p_refs[j][:, w * MLSTM_W + h * dh:w * MLSTM_W + (h + 1) * dh]

    q, k, v = ([part(i, w) for i in ids] for w in range(3))
    gcol = [gc_refs[j][...] + bc_ref[...] for j in range(ns)]
    grow = [gr_refs[j][...] + br_ref[...] for j in range(ns)]
    bcum_c = [_split_dot(tri, _neg_softplus(-g)) for g in gcol]
    bcum_r = [_split_dot_r(_neg_softplus(-g), tri_t) for g in grow]
    i_c = [rep(gcol[j][:, h:h + 1]) for j, h in chains]
    b_c = [rep(bcum_c[j][:, nh + h:nh + h + 1]) for j, h in chains]
    i_r = [grow[j][h:h + 1, :] for j, h in chains]
    b_r = [bcum_r[j][nh + h:nh + h + 1, :] for j, h in chains]
    m_prev = [m_scr[i] for i in ids]
    c_prev = [c_scr[i] for i in ids]
    n_prev = [n_scr[i] for i in ids]

    s = [_dot_nt(q[i], k[i]) for i in ids]
    q_c = [_dot(q[i], c_prev[i].astype(BF16)) for i in ids]

    for i in ids:
        b_tot = b_c[i][L - 1:L, :]
        d_state = b_tot - b_c[i] + i_c[i]
        m_new = jnp.maximum(b_tot + m_prev[i], jnp.max(d_state, axis=0, keepdims=True))
        kw = k[i].astype(F32) * (jnp.exp(d_state - m_new) * scale)
        decay = jnp.exp(b_tot + m_prev[i] - m_new)
        c_scr[i] = decay * c_prev[i] + _dot_tn(kw.astype(BF16), v[i])
        n_scr[i] = decay * n_prev[i] + jnp.sum(kw, axis=0, keepdims=True)
        m_scr[i] = m_new

    w_inter, m_t, qk = [], [], []
    for i in ids:
        d_intra = jnp.where(causal, b_c[i] - b_r[i] + i_r[i], NEG_BIG)
        d_inter = b_c[i] + m_prev[i]
        m_t.append(jnp.maximum(d_inter, jnp.max(d_intra, axis=-1, keepdims=True)))
        w_inter.append(jnp.exp(d_inter - m_t[i]))
        qk.append(s[i] * scale * jnp.exp(d_intra - m_t[i]))
    qk_v = [_dot(qk[i].astype(BF16), v[i]) for i in ids]

    for i, (j, h) in enumerate(chains):
        cols = slice(h * dh, (h + 1) * dh)
        num = w_inter[i] * q_c[i] + qk_v[i]
        den = (w_inter[i] * jnp.sum(q[i].astype(F32) * n_prev[i], axis=-1, keepdims=True)
               + jnp.sum(qk[i], axis=-1, keepdims=True))
        h_tilde = num / jnp.maximum(jnp.abs(den), jnp.exp(-m_t[i]))
        cell = jax.nn.sigmoid(part(i, 3).astype(F32)) * h_tilde
        mu = jnp.mean(cell, axis=-1, keepdims=True)
        cc = cell - mu
        var = jnp.mean(cc * cc, axis=-1, keepdims=True)
        y_ref[0, j, :, cols] = (cc * lax.rsqrt(var + LN_EPS) * ng_ref[:, cols]).astype(BF16)


def _mlstm(p, gcol, grow, b_i, b_f, norm_g, b, s):
    L, ns = MLSTM_CHUNK, MLSTM_SEQS
    nc = s // L
    chains = ns * MLSTM_HEADS
    bias = jnp.concatenate([b_i, b_f]).astype(F32)
    bias_c = jnp.zeros((1, LANES), F32).at[0, :GATE_COLS].set(bias)
    bias_r = bias.reshape(GATE_COLS, 1)
    tok = lambda j: (lambda bi, c: ((bi * ns + j) * nc + c, 0))
    tok_t = lambda j: (lambda bi, c: (0, (bi * ns + j) * nc + c))
    const = lambda bi, c: (0, 0)
    return pl.pallas_call(
        _mlstm_body,
        grid=(b // ns, nc),
        in_specs=(
            [pl.BlockSpec((L, 4 * MLSTM_W), tok(j)) for j in range(ns)]
            + [pl.BlockSpec((L, LANES), tok(j)) for j in range(ns)]
            + [pl.BlockSpec((GATE_COLS, L), tok_t(j)) for j in range(ns)]
            + [pl.BlockSpec((1, LANES), const), pl.BlockSpec((GATE_COLS, 1), const),
               pl.BlockSpec((1, MLSTM_W), const)]),
        out_specs=pl.BlockSpec((1, ns, L, MLSTM_W), lambda bi, c: (bi, 0, c, 0)),
        out_shape=jax.ShapeDtypeStruct((b // ns, ns, s, MLSTM_W), BF16),
        scratch_shapes=[
            pltpu.VMEM((chains, MLSTM_HEAD_DIM, MLSTM_HEAD_DIM), F32),
            pltpu.VMEM((chains, 1, MLSTM_HEAD_DIM), F32),
            pltpu.VMEM((chains, 1, LANES), F32),
        ],
        compiler_params=_params("parallel", "arbitrary"),
        name="mlstm",
    )(*([p] * ns + [gcol] * ns + [grow] * ns), bias_c, bias_r,
      norm_g.reshape(1, MLSTM_W)).reshape(b * s, MLSTM_W)


def _stick_body(bias_ref, q_ref, k_ref, v_ref, y_ref, acc_scr, c_scr):
    t, w = SB_TILE, SB_WINDOW
    units = range(SB_UNITS)
    starts = [(pl.program_id(2) * SB_UNITS + u) * t for u in units]

    def suffix(n):
        return (lax.broadcasted_iota(jnp.int32, (n, n), 0)
                >= lax.broadcasted_iota(jnp.int32, (n, n), 1)).astype(BF16)

    suffix_t, suffix_w = suffix(t), suffix(w)
    low = lax.broadcasted_iota(jnp.int32, (t, LANES), 1) < SB_HEAD_DIM

    def stacked(q2):
        zero = jnp.zeros_like(q2)
        return jnp.concatenate([jnp.where(low, q2, zero), jnp.where(low, zero, q2)], axis=0)

    def keys(first, n):
        return pl.ds(pl.multiple_of(first, t), n)

    def weights(z, incl, spent_later):
        return jnp.exp2(z - incl - spent_later).astype(BF16)

    def spent(sp):
        return jnp.sum(sp, axis=-1, keepdims=True)

    qs = [stacked(_scaled_q(q_ref[0, u * t:(u + 1) * t, :], LOG2E / math.sqrt(SB_HEAD_DIM))) for u in units]
    win = [jnp.maximum(s0 - w, 0) for s0 in starts]
    causal = bias_ref[...]
    causal = jnp.concatenate([causal, causal], axis=0)
    col_w = lax.broadcasted_iota(jnp.int32, (1, w), 1)
    earlier = [jnp.where(col_w < starts[u] - win[u], 0.0, NEG_BIG) for u in units]
    z_diag = [_dot_nt(qs[u], k_ref[0, keys(starts[u], t), :]) + causal for u in units]
    z_prev = [_dot_nt(qs[u], k_ref[0, keys(win[u], w), :]) + earlier[u] for u in units]
    sp_diag = [_softplus2(z) for z in z_diag]
    sp_prev = [_softplus2(z) for z in z_prev]
    in_diag = [_split_dot_r(sp, suffix_t) for sp in sp_diag]
    in_prev = [_split_dot_r(sp, suffix_w) for sp in sp_prev]
    c_diag = [spent(sp) for sp in sp_diag]
    a_diag = [weights(z_diag[u], in_diag[u], 0.0) for u in units]
    a_prev = [weights(z_prev[u], in_prev[u], c_diag[u]) for u in units]
    c_both = [c_diag[u] + spent(sp_prev[u]) for u in units]
    for u in units:
        acc_scr[u] = (_dot(a_diag[u], v_ref[0, keys(starts[u], t), :])
                      + _dot(a_prev[u], v_ref[0, keys(win[u], w), :]))
        c_scr[u] = jnp.broadcast_to(c_both[u], c_scr.shape[1:])

    def cond(carry):
        j, c_min = carry
        return (j >= 0) & (c_min < SB_SPENT_LOG2)

    for u in units:
        def body(carry, u=u):
            j, _ = carry
            c_old = c_scr[u]
            z = _dot_nt(qs[u], k_ref[0, keys(j * t, t), :])
            sp = _softplus2(z)
            a = weights(z, _split_dot_r(sp, suffix_t), jnp.concatenate([c_old] * (t // LANES), axis=1))
            acc_scr[u] += _dot(a, v_ref[0, keys(j * t, t), :])
            c_new = c_old + spent(sp)
            c_scr[u] = c_new
            return j - 1, jnp.min(c_new)

        lax.while_loop(cond, body, (win[u] // t - 1, jnp.min(c_both[u])))
        out = acc_scr[u]
        y_ref[0, u * t:(u + 1) * t, :] = jnp.where(low, out[:t], out[t:]).astype(BF16)


def _stick_breaking(p, b, s, col0):
    t = SB_TILE
    step = SB_UNITS * t
    pairs = SB_W // LANES
    pv = p.reshape(b, s, p.shape[1])
    c0 = col0 // LANES
    idx = np.arange(t)
    causal = jnp.asarray(np.where(idx[None, :] < idx[:, None], 0.0, NEG_BIG), F32)
    return pl.pallas_call(
        _stick_body,
        grid=(b, pairs, s // step),
        in_specs=[
            pl.BlockSpec((t, t), lambda bi, hp, qi: (0, 0)),
            pl.BlockSpec((1, step, LANES), lambda bi, hp, qi: (bi, qi, c0 + hp)),
            pl.BlockSpec((1, s, LANES), lambda bi, hp, qi: (bi, 0, c0 + pairs + hp)),
            pl.BlockSpec((1, s, LANES), lambda bi, hp, qi: (bi, 0, c0 + 2 * pairs + hp)),
        ],
        out_specs=pl.BlockSpec((1, step, LANES), lambda bi, hp, qi: (bi, qi, hp)),
        out_shape=jax.ShapeDtypeStruct((b, s, SB_W), BF16),
        scratch_shapes=[pltpu.VMEM((SB_UNITS, 2 * t, LANES), F32), pltpu.VMEM((SB_UNITS, 2 * t, LANES), F32)],
        compiler_params=_params("parallel", "parallel", "arbitrary"),
        name="stick_breaking",
    )(causal, pv, pv, pv).reshape(b * s, SB_W)


def _outproj_odd_body(ym_ref, ys_ref, wo_ref, x_ref, g_ref, b_ref, win_ref, wout_ref, g3_ref, b3_ref, o_ref):
    mix = _dot(ym_ref[...], wo_ref[:MLSTM_W, :]) + _dot(ys_ref[...], wo_ref[MLSTM_W:, :])
    mixed = _layer_norm(ALPHA * x_ref[...] + mix, g_ref[...], b_ref[...])
    o_ref[...] = _ffn_block(mixed, win_ref, wout_ref, g3_ref[...], b3_ref[...])


def _outproj_odd(y_m, y_s, w_out, x, g, b, ffn):
    n, d = x.shape
    ffn_w_in, ffn_w_out, ffn_g, ffn_b = ffn
    tm = TOKEN_TILE
    return pl.pallas_call(
        _outproj_odd_body,
        grid=(n // tm,),
        in_specs=[
            pl.BlockSpec((tm, MLSTM_W), lambda i: (i, 0)),
            pl.BlockSpec((tm, SB_W), lambda i: (i, 0)),
            pl.BlockSpec((d, d), lambda i: (0, 0), pipeline_mode=pl.Buffered(1)),
            pl.BlockSpec((tm, d), lambda i: (i, 0)),
            pl.BlockSpec((1, d), lambda i: (0, 0)),
            pl.BlockSpec((1, d), lambda i: (0, 0)),
        ] + _ffn_specs(d, ffn_w_out.shape[0]),
        out_specs=pl.BlockSpec((tm, d), lambda i: (i, 0)),
        out_shape=jax.ShapeDtypeStruct((n, d), F32),
        compiler_params=_params("parallel"),
        name="outproj_odd_ffn",
    )(y_m, y_s, w_out, x, g.reshape(1, d), b.reshape(1, d),
      ffn_w_in, ffn_w_out, ffn_g.reshape(1, d), ffn_b.reshape(1, d))


def _even_mixer(x, w_in, conv_w, w_out, g, b, batch, seq, ffn):
    pc, pq = _proj_even(x, w_in.astype(BF16))
    y_att = _dilated_attention(pq, batch, seq)
    return _outproj_even(pc, y_att, conv_w, w_out.astype(BF16), x, g, b, seq, ffn)


def _odd_mixer(x, w_in, b_i, b_f, norm_g, w_out, g, b, batch, seq, ffn):
    gate0 = 4 * MLSTM_W
    w_main = jnp.concatenate([w_in[:, :gate0], w_in[:, gate0 + GATE_COLS:]], axis=1).astype(BF16)
    w_gate = w_in[:, gate0:gate0 + GATE_COLS].astype(BF16)
    w_gate_pad = jnp.pad(w_gate, ((0, 0), (0, LANES - GATE_COLS)))
    p, gcol, grow = _proj_gates(x, w_main, w_gate_pad, w_gate.T)
    y_m = _mlstm(p, gcol, grow, b_i, b_f, norm_g, batch, seq)
    y_s = _stick_breaking(p, batch, seq, gate0)
    return _outproj_odd(y_m, y_s, w_out.astype(BF16), x, g, b, ffn)


def kernel(x, l0_ffn1_w_in, l0_ffn1_w_out, l0_ln1_g, l0_ln1_b, l0_mix_w_in, l0_conv_w, l0_mix_w_out, l0_ln2_g, l0_ln2_b, l0_ffn2_w_in, l0_ffn2_w_out, l0_ln3_g, l0_ln3_b, l1_ffn1_w_in, l1_ffn1_w_out, l1_ln1_g, l1_ln1_b, l1_mix_w_in, l1_mlstm_b_i, l1_mlstm_b_f, l1_mlstm_norm_g, l1_mix_w_out, l1_ln2_g, l1_ln2_b, l1_ffn2_w_in, l1_ffn2_w_out, l1_ln3_g, l1_ln3_b):
    batch, seq, d = x.shape
    ffn = lambda w_in, w_out, g, b: (w_in.astype(BF16), w_out.astype(BF16), g, b)
    t = x.reshape(batch * seq, d)
    t = _ffn_ln(t, ffn(l0_ffn1_w_in, l0_ffn1_w_out, l0_ln1_g, l0_ln1_b))
    t = _even_mixer(t, l0_mix_w_in, l0_conv_w, l0_mix_w_out, l0_ln2_g, l0_ln2_b, batch, seq,
                    ffn(l0_ffn2_w_in, l0_ffn2_w_out, l0_ln3_g, l0_ln3_b))
    t = _ffn_ln(t, ffn(l1_ffn1_w_in, l1_ffn1_w_out, l1_ln1_g, l1_ln1_b))
    t = _odd_mixer(t, l1_mix_w_in, l1_mlstm_b_i, l1_mlstm_b_f, l1_mlstm_norm_g, l1_mix_w_out,
                   l1_ln2_g, l1_ln2_b, batch, seq,
                   ffn(l1_ffn2_w_in, l1_ffn2_w_out, l1_ln3_g, l1_ln3_b))
    return t.reshape(batch, seq, d)
```

```python
import functools
import math

import jax
import jax.numpy as jnp
import numpy as np
from jax import lax
from jax.experimental import pallas as pl
from jax.experimental.pallas import tpu as pltpu

F32 = jnp.float32
BF16 = jnp.bfloat16

DEPTH = 2
ALPHA = (2 * DEPTH) ** 0.25
FFN_RES_WEIGHT = 0.5
LN_EPS = 1e-5

LANES = 128
MXU_TILE = 256
BF16_SUBLANES = 16
BLK = 128
CONV_CH = 256
CONV_WIDTH = 3
DSWA_HEAD_DIM = 64
DSWA_W = 768
DSWA_SPAN = 128
DSWA_PERIOD = 16
DSWA_SUPER = DSWA_PERIOD * BLK
D1_UNITS = 4
MLSTM_HEADS = 4
MLSTM_W = 512
MLSTM_HEAD_DIM = 128
MLSTM_CHUNK = 128
MLSTM_SEQS = 1
SB_HEAD_DIM = 64
SB_W = 512
GATE_COLS = 2 * MLSTM_HEADS

TOKEN_TILE = 512
FFN_TOKEN_TILE = 1024
SB_TILE = 128
SB_WINDOW = 256
SB_UNITS = 4
SB_SPENT_LOG2 = 127.0
LOG2E = math.log2(math.e)
NEG_BIG = -1e30
VMEM_LIMIT = 56 * 1024 * 1024


def _params(*sem):
    return pltpu.CompilerParams(dimension_semantics=sem, vmem_limit_bytes=VMEM_LIMIT)


def _layer_norm(z, g, b):
    mu = jnp.mean(z, axis=-1, keepdims=True)
    zc = z - mu
    var = jnp.mean(zc * zc, axis=-1, keepdims=True)
    return zc * lax.rsqrt(var + LN_EPS) * g + b


def _dot(a, b):
    return jnp.dot(a, b, preferred_element_type=F32)


def _dot_nt(a, b):
    return lax.dot_general(a, b, (((1,), (1,)), ((), ())), preferred_element_type=F32)


def _dot_tn(a, b):
    return lax.dot_general(a, b, (((0,), (0,)), ((), ())), preferred_element_type=F32)


def _split_dot(tri, x):
    hi = x.astype(BF16)
    lo = (x - hi.astype(F32)).astype(BF16)
    return _dot(jnp.concatenate([tri, tri], axis=1), jnp.concatenate([hi, lo], axis=0))


def _split_dot_r(x, tri):
    hi = x.astype(BF16)
    lo = (x - hi.astype(F32)).astype(BF16)
    return _dot(jnp.concatenate([hi, lo], axis=1), jnp.concatenate([tri, tri], axis=0))


def _neg_softplus(z):
    return -(jnp.maximum(z, 0.0) + jnp.log(1.0 + jnp.exp(-jnp.abs(z))))


def _softplus2(z2):
    return jnp.maximum(z2, 0.0) + jnp.log(1.0 + jnp.exp2(-jnp.abs(z2))) * LOG2E


def _scaled_q(q, scale):
    return (q.astype(F32) * scale).astype(BF16)


def _cast_blocks(rows, steps):
    return next(n for n in (64, 32, 16, 8, 4, 2, 1) if n <= steps and rows % (n * BF16_SUBLANES) == 0)


def _cast_specs(weights, step_of, steps):
    specs, shapes = [], []
    for w in weights:
        blocks = _cast_blocks(w.shape[0], steps)
        index = lambda *ids, blocks=blocks: (jnp.minimum(step_of(*ids), blocks - 1), 0)
        specs.append(pl.BlockSpec((w.shape[0] // blocks, w.shape[1]), index))
        shapes.append(jax.ShapeDtypeStruct(w.shape, BF16))
    return specs, shapes


def _cast_rows(src_refs, dst_refs):
    for src, dst in zip(src_refs, dst_refs):
        dst[...] = src[...].astype(BF16)


def _ffn_chunks(d_ff):
    tiles = d_ff // MXU_TILE
    cut = (tiles + 1) // 2 * MXU_TILE
    return ((0, cut), (cut, d_ff))


def _ffn_block(x, win_ref, wout_ref, g, b):
    d_ff = wout_ref.shape[0]
    halves = range(2)
    r = x.shape[0] // 2
    xs = [x[h * r:(h + 1) * r] for h in halves]
    xb = [t.astype(BF16) for t in xs]
    acc = [None, None]
    for lo, hi in _ffn_chunks(d_ff):
        gate_up = [(_dot(xb[h], win_ref[:, lo:hi]), _dot(xb[h], win_ref[:, d_ff + lo:d_ff + hi])) for h in halves]
        for h in halves:
            gate, up = gate_up[h]
            act = (gate * jax.nn.sigmoid(gate) * up).astype(BF16)
            part = _dot(act, wout_ref[lo:hi, :])
            acc[h] = part if acc[h] is None else acc[h] + part
    return jnp.concatenate([_layer_norm(ALPHA * xs[h] + FFN_RES_WEIGHT * acc[h], g, b) for h in halves], axis=0)


def _ffn_specs(d, d_ff):
    const = dict(pipeline_mode=pl.Buffered(1))
    return [
        pl.BlockSpec((d, 2 * d_ff), lambda i: (0, 0), **const),
        pl.BlockSpec((d_ff, d), lambda i: (0, 0), **const),
        pl.BlockSpec((1, d), lambda i: (0, 0)),
        pl.BlockSpec((1, d), lambda i: (0, 0)),
    ]


def _ffn_ln_body(x_ref, win_ref, wout_ref, g_ref, b_ref, o_ref):
    o_ref[...] = _ffn_block(x_ref[...], win_ref, wout_ref, g_ref[...], b_ref[...])


def _ffn_ln(x, ffn):
    n, d = x.shape
    w_in, w_out, g, b = ffn
    d_ff = w_out.shape[0]
    tm = FFN_TOKEN_TILE
    return pl.pallas_call(
        _ffn_ln_body,
        grid=(n // tm,),
        in_specs=[pl.BlockSpec((tm, d), lambda i: (i, 0))] + _ffn_specs(d, d_ff),
        out_specs=pl.BlockSpec((tm, d), lambda i: (i, 0)),
        out_shape=jax.ShapeDtypeStruct((n, d), F32),
        compiler_params=_params("parallel"),
        name="ffn_ln",
    )(x, w_in, w_out, g.reshape(1, d), b.reshape(1, d))


def _proj_even_body(*refs, tm, tn, n_cast):
    x_ref, perm_ref, w_ref = refs[:3]
    pc_ref, pq_ref = refs[3 + n_cast:5 + n_cast]
    _cast_rows(refs[3:3 + n_cast], refs[5 + n_cast:])
    cw = 3 * CONV_CH
    xb = x_ref[...].astype(BF16)
    pc_ref[...] = _dot(xb, w_ref[:, :cw]).astype(BF16)
    xg = _dot(perm_ref[...], xb).astype(BF16)
    rows = tm // DSWA_PERIOD
    for c in range((w_ref.shape[1] - cw) // tn):
        res = _dot(xg, w_ref[:, cw + c * tn:cw + (c + 1) * tn]).astype(BF16)
        pq_ref[0, :, :, c * tn:(c + 1) * tn] = res.reshape(DSWA_PERIOD, rows, tn)


def _proj_even(x, w, casts):
    n, d = x.shape
    cw = 3 * CONV_CH
    qw = w.shape[1] - cw
    tm = TOKEN_TILE
    tiles = DSWA_SUPER // tm
    rows = tm // DSWA_PERIOD
    dst = np.arange(tm)
    perm = np.zeros((tm, tm), np.float32)
    perm[dst, (dst % rows) * DSWA_PERIOD + dst // rows] = 1.0
    const = dict(pipeline_mode=pl.Buffered(1))
    cast_specs, cast_shapes = _cast_specs(casts, lambda i: i, n // tm)
    pc, pq, *cast = pl.pallas_call(
        functools.partial(_proj_even_body, tm=tm, tn=768, n_cast=len(casts)),
        grid=(n // tm,),
        in_specs=[
            pl.BlockSpec((tm, d), lambda i: (i, 0)),
            pl.BlockSpec((tm, tm), lambda i: (0, 0), **const),
            pl.BlockSpec((d, cw + qw), lambda i: (0, 0), **const),
        ] + cast_specs,
        out_specs=[
            pl.BlockSpec((tm, cw), lambda i: (i, 0)),
            pl.BlockSpec((1, DSWA_PERIOD, rows, qw), lambda i: (i // tiles, 0, i % tiles, 0)),
        ] + cast_specs,
        out_shape=[
            jax.ShapeDtypeStruct((n, cw), BF16),
            jax.ShapeDtypeStruct((n // DSWA_SUPER, DSWA_PERIOD, DSWA_SUPER // DSWA_PERIOD, qw), BF16),
        ] + cast_shapes,
        compiler_params=_params("arbitrary"),
        name="proj_even",
    )(x, jnp.asarray(perm, BF16), w, *casts)
    return pc, pq, cast


def _proj_gates_body(x_ref, w_ref, wg_ref, wgt_ref, o_ref, gc_ref, gr_ref, *, tn):
    xb = x_ref[...].astype(BF16)
    for c in range(w_ref.shape[1] // tn):
        o_ref[:, c * tn:(c + 1) * tn] = _dot(xb, w_ref[:, c * tn:(c + 1) * tn]).astype(BF16)
    gc_ref[...] = _dot(xb, wg_ref[...])
    gr_ref[...] = _dot_nt(wgt_ref[...], xb)


def _proj_gates(x, w, wg, wgt):
    n, d = x.shape
    p = w.shape[1]
    tm = TOKEN_TILE
    const = dict(pipeline_mode=pl.Buffered(1))
    return pl.pallas_call(
        functools.partial(_proj_gates_body, tn=512),
        grid=(n // tm,),
        in_specs=[
            pl.BlockSpec((tm, d), lambda i: (i, 0)),
            pl.BlockSpec((d, p), lambda i: (0, 0), **const),
            pl.BlockSpec((d, LANES), lambda i: (0, 0), **const),
            pl.BlockSpec((GATE_COLS, d), lambda i: (0, 0), **const),
        ],
        out_specs=[
            pl.BlockSpec((tm, p), lambda i: (i, 0)),
            pl.BlockSpec((tm, LANES), lambda i: (i, 0)),
            pl.BlockSpec((GATE_COLS, tm), lambda i: (0, i)),
        ],
        out_shape=[
            jax.ShapeDtypeStruct((n, p), BF16),
            jax.ShapeDtypeStruct((n, LANES), F32),
            jax.ShapeDtypeStruct((GATE_COLS, n), F32),
        ],
        compiler_params=_params("parallel"),
        name="proj_odd",
    )(x, w, wg, wgt)


def _dilated_biases():
    def make(q_pos, k_pos, k_is_prev):
        rel = q_pos[:, None] - k_pos[None, :]
        ok = (rel >= 0) & (rel <= DSWA_SPAN)
        variants = [ok, ok & ~k_is_prev[None, :]]
        return jnp.asarray(np.stack([np.where(v, 0.0, NEG_BIG) for v in variants]), F32)

    per = DSWA_PERIOD
    kj = np.arange(2 * BLK)
    b16 = make(np.arange(BLK), kj - BLK, kj < BLK)
    qa, qi = np.divmod(np.arange(BLK), BLK // 4)
    ka, ki = np.divmod(np.arange(2 * BLK), 2 * BLK // 4)
    b4 = make(4 * qi + qa, 4 * (ki - BLK // 4) + ka, ki < BLK // 4)
    qg, qi = np.divmod(np.arange(BLK), BLK // per)
    kg, ki = np.divmod(np.arange(2 * BLK), 2 * BLK // per)
    b1 = make(per * qi + qg, per * (ki - BLK // per) + kg, ki < BLK // per)
    return b16, b4, b1


def _dilated_body(b16_ref, b4_ref, b1_ref, q_ref, kp_ref, kc_ref, vp_ref, vc_ref, y_ref,
                  kf, vf, acc, m_s, l_s, nat, q32, k32, v32):
    per = DSWA_PERIOD
    first = jnp.where(pl.program_id(1) == 0, 1, 0)
    kf[:, :BLK, :] = kp_ref[0]
    kf[:, BLK:, :] = kc_ref[0]
    vf[:, :BLK, :] = vp_ref[0]
    vf[:, BLK:, :] = vc_ref[0]
    scale = LOG2E / math.sqrt(DSWA_HEAD_DIM)

    def low_mask(rows):
        return lax.broadcasted_iota(jnp.int32, (rows, LANES), 1) < DSWA_HEAD_DIM

    def attend(units):
        scores = []
        for q2, kk, _, _ in units:
            low = low_mask(q2.shape[0])
            zero = jnp.zeros_like(q2)
            stacked = jnp.concatenate([jnp.where(low, q2, zero), jnp.where(low, zero, q2)], axis=0)
            scores.append(_dot_nt(stacked, kk))
        soft = []
        for s, (_, _, _, bias) in zip(scores, units):
            s = s + jnp.concatenate([bias, bias], axis=0)
            m = jnp.max(s, axis=-1, keepdims=True)
            p = jnp.exp2(s - m)
            soft.append((m, jnp.sum(p, axis=-1, keepdims=True), p.astype(BF16)))
        return [(m, l, _dot(p, vv)) for (m, l, p), (_, _, vv, _) in zip(soft, units)]

    def by_lane(x):
        r = x.shape[0] // 2
        return jnp.where(low_mask(r), x[:r], x[r:])

    def merge(part, m_old, l_old, acc_old):
        m, l, pv = (by_lane(x) for x in part)
        m_tot = jnp.maximum(m_old, m)
        a_old = jnp.exp2(m_old - m_tot)
        a_new = jnp.exp2(m - m_tot)
        return m_tot, l_old * a_old + l * a_new, acc_old * a_old + pv * a_new

    def gather(ref, lead, groups, rows):
        return jnp.concatenate([ref[lead + (g, rows, slice(None))] for g in groups], axis=0)

    def scatter(ref, groups, rows, value):
        n = value.shape[0] // len(groups)
        for j, g in enumerate(groups):
            ref[g, rows, :] = value[j * n:(j + 1) * n]

    state = (m_s, l_s, acc)

    def fold(parts, where):
        return [merge(part, *(gather(ref, (), groups, rows) for ref in state))
                for part, (groups, rows) in zip(parts, where)]

    def d16(it, carry):
        groups = [8 * it + j for j in range(8)]
        parts = attend([(_scaled_q(q_ref[0, g], scale), kf[g], vf[g], b16_ref[first]) for g in groups])
        for g, part in zip(groups, parts):
            for ref, x in zip(state, part):
                ref[g] = by_lane(x)
        return carry

    lax.fori_loop(0, per // 8, d16, 0)

    def d4(r4, carry):
        nq = BLK // 4
        groups = [4 * a + r4 for a in range(4)]
        units, where = [], []
        for blk in range(4):
            qrows = slice(blk * nq, (blk + 1) * nq)
            krows = slice(BLK - nq + blk * nq, BLK + (blk + 1) * nq)
            bias = b4_ref[first] if blk == 0 else b4_ref[0]
            units.append((_scaled_q(gather(q_ref, (0,), groups, qrows), scale),
                          gather(kf, (), groups, krows), gather(vf, (), groups, krows), bias))
            where.append((groups, qrows))
        for merged, (groups, rows) in zip(fold(attend(units), where), where):
            for ref, x in zip(state, merged):
                scatter(ref, groups, rows, x)
        return carry

    lax.fori_loop(0, 4, d4, 0)

    q32[...] = q_ref[0].astype(F32)
    k32[:, :BLK, :] = kp_ref[0].astype(F32)
    k32[:, BLK:, :] = kc_ref[0].astype(F32)
    v32[:, :BLK, :] = vp_ref[0].astype(F32)
    v32[:, BLK:, :] = vc_ref[0].astype(F32)

    def d1(it, carry):
        nq = BLK // per
        groups = list(range(per))
        units, where, blks = [], [], []
        for j in range(D1_UNITS):
            blk = D1_UNITS * it + j
            qrows = pl.ds(pl.multiple_of(blk * nq, nq), nq)
            krows = pl.ds(pl.multiple_of(BLK - nq + blk * nq, nq), 2 * nq)
            flag = jnp.where(blk == 0, first, 0)
            units.append((_scaled_q(gather(q32, (), groups, qrows), scale),
                          gather(k32, (), groups, krows).astype(BF16),
                          gather(v32, (), groups, krows).astype(BF16), b1_ref[flag]))
            where.append((groups, qrows))
            blks.append(blk)
        for j, (blk, (_, l, a)) in enumerate(zip(blks, fold(attend(units), where))):
            r = a.shape[0]
            out = a / l
            for g in groups:
                nat[j, pl.ds(g, nq, stride=per), :] = out[g * nq:(g + 1) * nq]
            y_ref[0, pl.ds(pl.multiple_of(blk * r, r), r), :] = nat[j].astype(BF16)
        return carry

    lax.fori_loop(0, DSWA_SUPER // (D1_UNITS * BLK), d1, 0)


def _dilated_attention(pq, b, s):
    per, w = DSWA_PERIOD, DSWA_W
    pairs = w // LANES
    nsb = s // DSWA_SUPER
    rows = DSWA_SUPER // per
    biases = _dilated_biases()
    blk = (1, per, rows, LANES)
    cur = lambda col: (lambda bi, sb, hp: (bi * nsb + sb, 0, 0, col * pairs + hp))
    prev = lambda col: (lambda bi, sb, hp: (bi * nsb + jnp.maximum(sb - 1, 0), 0, 0, col * pairs + hp))
    const = lambda a: pl.BlockSpec(a.shape, lambda bi, sb, hp: (0, 0, 0))
    return pl.pallas_call(
        _dilated_body,
        grid=(b, nsb, pairs),
        in_specs=[const(a) for a in biases] + [
            pl.BlockSpec(blk, cur(0)),
            pl.BlockSpec(blk, prev(1)),
            pl.BlockSpec(blk, cur(1)),
            pl.BlockSpec(blk, prev(2)),
            pl.BlockSpec(blk, cur(2)),
        ],
        out_specs=pl.BlockSpec((1, DSWA_SUPER, LANES), lambda bi, sb, hp: (bi * nsb + sb, 0, hp)),
        out_shape=jax.ShapeDtypeStruct((b * nsb, DSWA_SUPER, w), BF16),
        scratch_shapes=[
            pltpu.VMEM((per, 2 * rows, LANES), BF16),
            pltpu.VMEM((per, 2 * rows, LANES), BF16),
            pltpu.VMEM((per, rows, LANES), F32),
            pltpu.VMEM((per, rows, LANES), F32),
            pltpu.VMEM((per, rows, LANES), F32),
            pltpu.VMEM((D1_UNITS, BLK, LANES), F32),
            pltpu.VMEM((per, rows, LANES), F32),
            pltpu.VMEM((per, 2 * rows, LANES), F32),
            pltpu.VMEM((per, 2 * rows, LANES), F32),
        ],
        compiler_params=_params("parallel", "parallel", "parallel"),
        name="dilated",
    )(*biases, pq, pq, pq, pq, pq).reshape(b * s, w)


def _outproj_even_body(p_ref, halo_ref, y_ref, cw_ref, wo_ref, x_ref, g_ref, b_ref,
                       win_ref, wout_ref, g3_ref, b3_ref, o_ref, *, seq, tm, halo):
    c = CONV_CH
    pc = p_ref[...].astype(F32)
    bg, u = pc[:, :c], pc[:, c:2 * c] * pc[:, 2 * c:3 * c]
    ph = halo_ref[...].astype(F32)
    uh = ph[:, c:2 * c] * ph[:, 2 * c:3 * c]
    starts_sequence = (pl.program_id(0) * tm) % seq == 0
    uh = jnp.where(starts_sequence, 0.0, uh)
    full = jnp.concatenate([uh, u], axis=0)
    cw = cw_ref[...]
    conv = cw[2:3, :] * u + cw[1:2, :] * full[halo - 1:halo - 1 + tm] + cw[0:1, :] * full[halo - 2:halo - 2 + tm]
    y_conv = (bg * conv).astype(BF16)
    mix = _dot(y_conv, wo_ref[:c, :]) + _dot(y_ref[...], wo_ref[c:, :])
    mixed = _layer_norm(ALPHA * x_ref[...] + mix, g_ref[...], b_ref[...])
    o_ref[...] = _ffn_block(mixed, win_ref, wout_ref, g3_ref[...], b3_ref[...])


def _outproj_even(p, y_att, conv_w, w_out, x, g, b, seq, ffn):
    n, d = x.shape
    ffn_w_in, ffn_w_out, ffn_g, ffn_b = ffn
    tm = TOKEN_TILE
    halo = 16
    cw = 3 * CONV_CH
    return pl.pallas_call(
        functools.partial(_outproj_even_body, seq=seq, tm=tm, halo=halo),
        grid=(n // tm,),
        in_specs=[
            pl.BlockSpec((tm, cw), lambda i: (i, 0)),
            pl.BlockSpec((halo, cw), lambda i: (jnp.maximum(i * (tm // halo) - 1, 0), 0)),
            pl.BlockSpec((tm, DSWA_W), lambda i: (i, 0)),
            pl.BlockSpec((CONV_WIDTH, CONV_CH), lambda i: (0, 0)),
            pl.BlockSpec((d, d), lambda i: (0, 0), pipeline_mode=pl.Buffered(1)),
            pl.BlockSpec((tm, d), lambda i: (i, 0)),
            pl.BlockSpec((1, d), lambda i: (0, 0)),
            pl.BlockSpec((1, d), lambda i: (0, 0)),
        ] + _ffn_specs(d, ffn_w_out.shape[0]),
        out_specs=pl.BlockSpec((tm, d), lambda i: (i, 0)),
        out_shape=jax.ShapeDtypeStruct((n, d), F32),
        compiler_params=_params("parallel"),
        name="outproj_even_ffn",
    )(p, p, y_att, conv_w, w_out, x, g.reshape(1, d), b.reshape(1, d),
      ffn_w_in, ffn_w_out, ffn_g.reshape(1, d), ffn_b.reshape(1, d))


def _mlstm_body(*refs):
    L, dh, nh, ns = MLSTM_CHUNK, MLSTM_HEAD_DIM, MLSTM_HEADS, MLSTM_SEQS
    p_refs, gc_refs, gr_refs = refs[:ns], refs[ns:2 * ns], refs[2 * ns:3 * ns]
    bc_ref, br_ref, ng_ref, y_ref, c_scr, n_scr, m_scr = refs[3 * ns:]

    @pl.when(pl.program_id(1) == 0)
    def _():
        c_scr[...] = jnp.zeros_like(c_scr)
        n_scr[...] = jnp.zeros_like(n_scr)
        m_scr[...] = jnp.zeros_like(m_scr)

    row = lax.broadcasted_iota(jnp.int32, (L, L), 0)
    col = lax.broadcasted_iota(jnp.int32, (L, L), 1)
    causal = col <= row
    tri = causal.astype(BF16)
    tri_t = (row <= col).astype(BF16)

    scale = 1.0 / math.sqrt(dh)
    rep = lambda a: jnp.broadcast_to(a, (L, LANES))

    chains = [(j, h) for j in range(ns) for h in range(nh)]
    ids = range(len(chains))

    def part(i, w):
        j, h = chains[i]
        return p_refs[j][:, w * MLSTM_W + h * dh:w * MLSTM_W + (h + 1) * dh]

    q, k, v = ([part(i, w) for i in ids] for w in range(3))
    gcol = [gc_refs[j][...] + bc_ref[...] for j in range(ns)]
    grow = [gr_refs[j][...] + br_ref[...] for j in range(ns)]
    bcum_c = [_split_dot(tri, _neg_softplus(-g)) for g in gcol]
    bcum_r = [_split_dot_r(_neg_softplus(-g), tri_t) for g in grow]
    i_c = [rep(gcol[j][:, h:h + 1]) for j, h in chains]
    b_c = [rep(bcum_c[j][:, nh + h:nh + h + 1]) for j, h in chains]
    i_r = [grow[j][h:h + 1, :] for j, h in chains]
    b_r = [bcum_r[j][nh + h:nh + h + 1, :] for j, h in chains]
    m_prev = [m_scr[i] for i in ids]
    c_prev = [c_scr[i] for i in ids]
    n_prev = [n_scr[i] for i in ids]

    s = [_dot_nt(q[i], k[i]) for i in ids]
    q_c = [_dot(q[i], c_prev[i].astype(BF16)) for i in ids]

    for i in ids:
        b_tot = b_c[i][L - 1:L, :]
        d_state = b_tot - b_c[i] + i_c[i]
        m_new = jnp.maximum(b_tot + m_prev[i], jnp.max(d_state, axis=0, keepdims=True))
        kw = k[i].astype(F32) * (jnp.exp(d_state - m_new) * scale)
        decay = jnp.exp(b_tot + m_prev[i] - m_new)
        c_scr[i] = decay * c_prev[i] + _dot_tn(kw.astype(BF16), v[i])
        n_scr[i] = decay * n_prev[i] + jnp.sum(kw, axis=0, keepdims=True)
        m_scr[i] = m_new

    w_inter, m_t, qk = [], [], []
    for i in ids:
        d_intra = jnp.where(causal, b_c[i] - b_r[i] + i_r[i], NEG_BIG)
        d_inter = b_c[i] + m_prev[i]
        m_t.append(jnp.maximum(d_inter, jnp.max(d_intra, axis=-1, keepdims=True)))
        w_inter.append(jnp.exp(d_inter - m_t[i]))
        qk.append(s[i] * scale * jnp.exp(d_intra - m_t[i]))
    qk_v = [_dot(qk[i].astype(BF16), v[i]) for i in ids]

    for i, (j, h) in enumerate(chains):
        cols = slice(h * dh, (h + 1) * dh)
        num = w_inter[i] * q_c[i] + qk_v[i]
        den = (w_inter[i] * jnp.sum(q[i].astype(F32) * n_prev[i], axis=-1, keepdims=True)
               + jnp.sum(qk[i], axis=-1, keepdims=True))
        h_tilde = num / jnp.maximum(jnp.abs(den), jnp.exp(-m_t[i]))
        cell = jax.nn.sigmoid(part(i, 3).astype(F32)) * h_tilde
        mu = jnp.mean(cell, axis=-1, keepdims=True)
        cc = cell - mu
        var = jnp.mean(cc * cc, axis=-1, keepdims=True)
        y_ref[0, j, :, cols] = (cc * lax.rsqrt(var + LN_EPS) * ng_ref[:, cols]).astype(BF16)


def _mlstm(p, gcol, grow, b_i, b_f, norm_g, b, s):
    L, ns = MLSTM_CHUNK, MLSTM_SEQS
    nc = s // L
    chains = ns * MLSTM_HEADS
    bias = jnp.concatenate([b_i, b_f]).astype(F32)
    bias_c = jnp.zeros((1, LANES), F32).at[0, :GATE_COLS].set(bias)
    bias_r = bias.reshape(GATE_COLS, 1)
    tok = lambda j: (lambda bi, c: ((bi * ns + j) * nc + c, 0))
    tok_t = lambda j: (lambda bi, c: (0, (bi * ns + j) * nc + c))
    const = lambda bi, c: (0, 0)
    return pl.pallas_call(
        _mlstm_body,
        grid=(b // ns, nc),
        in_specs=(
            [pl.BlockSpec((L, 4 * MLSTM_W), tok(j)) for j in range(ns)]
            + [pl.BlockSpec((L, LANES), tok(j)) for j in range(ns)]
            + [pl.BlockSpec((GATE_COLS, L), tok_t(j)) for j in range(ns)]
            + [pl.BlockSpec((1, LANES), const), pl.BlockSpec((GATE_COLS, 1), const),
               pl.BlockSpec((1, MLSTM_W), const)]),
        out_specs=pl.BlockSpec((1, ns, L, MLSTM_W), lambda bi, c: (bi, 0, c, 0)),
        out_shape=jax.ShapeDtypeStruct((b // ns, ns, s, MLSTM_W), BF16),
        scratch_shapes=[
            pltpu.VMEM((chains, MLSTM_HEAD_DIM, MLSTM_HEAD_DIM), F32),
            pltpu.VMEM((chains, 1, MLSTM_HEAD_DIM), F32),
            pltpu.VMEM((chains, 1, LANES), F32),
        ],
        compiler_params=_params("parallel", "arbitrary"),
        name="mlstm",
    )(*([p] * ns + [gcol] * ns + [grow] * ns), bias_c, bias_r,
      norm_g.reshape(1, MLSTM_W)).reshape(b * s, MLSTM_W)


def _stick_body(bias_ref, q_ref, k_ref, v_ref, y_ref, acc_scr, c_scr):
    t, w = SB_TILE, SB_WINDOW
    units = range(SB_UNITS)
    starts =[(pl.program_id(2) * SB_UNITS + u) * t for u in units]

    def suffix(n):
        return (lax.broadcasted_iota(jnp.int32, (n, n), 0)
                >= lax.broadcasted_iota(jnp.int32, (n, n), 1)).astype(BF16)

    suffix_t, suffix_w = suffix(t), suffix(w)
    low = lax.broadcasted_iota(jnp.int32, (t, LANES), 1) < SB_HEAD_DIM

    def stacked(q2):
        zero = jnp.zeros_like(q2)
        return jnp.concatenate([jnp.where(low, q2, zero), jnp.where(low, zero, q2)], axis=0)

    def keys(first, n):
        return pl.ds(pl.multiple_of(first, t), n)

    def weights(z, incl, spent_later):
        return jnp.exp2(z - incl - spent_later).astype(BF16)

    def spent(sp):
        return jnp.sum(sp, axis=-1, keepdims=True)

    qs = [stacked(_scaled_q(q_ref[0, u * t:(u + 1) * t, :], LOG2E / math.sqrt(SB_HEAD_DIM))) for u in units]
    win = [jnp.maximum(s0 - w, 0) for s0 in starts]
    causal = bias_ref[...]
    causal = jnp.concatenate([causal, causal], axis=0)
    col_w = lax.broadcasted_iota(jnp.int32, (1, w), 1)
    earlier = [jnp.where(col_w < starts[u] - win[u], 0.0, NEG_BIG) for u in units]
    z_diag = [_dot_nt(qs[u], k_ref[0, keys(starts[u], t), :]) + causal for u in units]
    z_prev = [_dot_nt(qs[u], k_ref[0, keys(win[u], w), :]) + earlier[u] for u in units]
    sp_diag = [_softplus2(z) for z in z_diag]
    sp_prev = [_softplus2(z) for z in z_prev]
    in_diag = [_split_dot_r(sp, suffix_t) for sp in sp_diag]
    in_prev = [_split_dot_r(sp, suffix_w) for sp in sp_prev]
    c_diag = [spent(sp) for sp in sp_diag]
    a_diag = [weights(z_diag[u], in_diag[u], 0.0) for u in units]
    a_prev = [weights(z_prev[u], in_prev[u], c_diag[u]) for u in units]
    c_both = [c_diag[u] + spent(sp_prev[u]) for u in units]
    for u in units:
        acc_scr[u] = (_dot(a_diag[u], v_ref[0, keys(starts[u], t), :])
                      + _dot(a_prev[u], v_ref[0, keys(win[u], w), :]))
        c_scr[u] = jnp.broadcast_to(c_both[u], c_scr.shape[1:])

    def cond(carry):
        j, c_min = carry
        return (j >= 0) & (c_min < SB_SPENT_LOG2)

    for u in units:
        def body(carry, u=u):
            j, _ = carry
            c_old = c_scr[u]
            z = _dot_nt(qs[u], k_ref[0, keys(j * t, t), :])
            sp = _softplus2(z)
            a = weights(z, _split_dot_r(sp, suffix_t), jnp.concatenate([c_old] * (t // LANES), axis=1))
            acc_scr[u] += _dot(a, v_ref[0, keys(j * t, t), :])
            c_new = c_old + spent(sp)
            c_scr[u] = c_new
            return j - 1, jnp.min(c_new)

        lax.while_loop(cond, body, (win[u] // t - 1, jnp.min(c_both[u])))
        out = acc_scr[u]
        y_ref[0, u * t:(u + 1) * t, :] = jnp.where(low, out[:t], out[t:]).astype(BF16)


def _stick_breaking(p, b, s, col0):
    t = SB_TILE
    step = SB_UNITS * t
    pairs = SB_W // LANES
    pv = p.reshape(b, s, p.shape[1])
    c0 = col0 // LANES
    idx = np.arange(t)
    causal = jnp.asarray(np.where(idx[None, :] < idx[:, None], 0.0, NEG_BIG), F32)
    return pl.pallas_call(
        _stick_body,
        grid=(b, pairs, s // step),
        in_specs=[
            pl.BlockSpec((t, t), lambda bi, hp, qi: (0, 0)),
            pl.BlockSpec((1, step, LANES), lambda bi, hp, qi: (bi, qi, c0 + hp)),
            pl.BlockSpec((1, s, LANES), lambda bi, hp, qi: (bi, 0, c0 + pairs + hp)),
            pl.BlockSpec((1, s, LANES), lambda bi, hp, qi: (bi, 0, c0 + 2 * pairs + hp)),
        ],
        out_specs=pl.BlockSpec((1, step, LANES), lambda bi, hp, qi: (bi, qi, hp)),
        out_shape=jax.ShapeDtypeStruct((b, s, SB_W), BF16),
        scratch_shapes=[pltpu.VMEM((SB_UNITS, 2 * t, LANES), F32), pltpu.VMEM((SB_UNITS, 2 * t, LANES), F32)],
        compiler_params=_params("parallel", "parallel", "arbitrary"),
        name="stick_breaking",
    )(causal, pv, pv, pv).reshape(b * s, SB_W)


def _outproj_odd_body(ym_ref, ys_ref, wo_ref, x_ref, g_ref, b_ref, win_ref, wout_ref, g3_ref, b3_ref, o_ref):
    mix = _dot(ym_ref[...], wo_ref[:MLSTM_W, :]) + _dot(ys_ref[...], wo_ref[MLSTM_W:, :])
    mixed = _layer_norm(ALPHA * x_ref[...] + mix, g_ref[...], b_ref[...])
    o_ref[...] = _ffn_block(mixed, win_ref, wout_ref, g3_ref[...], b3_ref[...])


def _outproj_odd(y_m, y_s, w_out, x, g, b, ffn):
    n, d = x.shape
    ffn_w_in, ffn_w_out, ffn_g, ffn_b = ffn
    tm = TOKEN_TILE
    return pl.pallas_call(
        _outproj_odd_body,
        grid=(n // tm,),
        in_specs=[
            pl.BlockSpec((tm, MLSTM_W), lambda i: (i, 0)),
            pl.BlockSpec((tm, SB_W), lambda i: (i, 0)),
            pl.BlockSpec((d, d), lambda i: (0, 0), pipeline_mode=pl.Buffered(1)),
            pl.BlockSpec((tm, d), lambda i: (i, 0)),
            pl.BlockSpec((1, d), lambda i: (0, 0)),
            pl.BlockSpec((1, d), lambda i: (0, 0)),
        ] + _ffn_specs(d, ffn_w_out.shape[0]),
        out_specs=pl.BlockSpec((tm, d), lambda i: (i, 0)),
        out_shape=jax.ShapeDtypeStruct((n, d), F32),
        compiler_params=_params("parallel"),
        name="outproj_odd_ffn",
    )(y_m, y_s, w_out, x, g.reshape(1, d), b.reshape(1, d),
      ffn_w_in, ffn_w_out, ffn_g.reshape(1, d), ffn_b.reshape(1, d))


def _even_mixer(x, w_in, conv_w, w_out, g, b, batch, seq, ffn, later_ffns):
    all_ffns = (ffn,) + tuple(later_ffns)
    pc, pq, cast = _proj_even(x, w_in.astype(BF16), [w for f in all_ffns for w in f[:2]])
    ready = [tuple(cast[2 * i:2 * i + 2]) + f[2:] for i, f in enumerate(all_ffns)]
    y_att = _dilated_attention(pq, batch, seq)
    return _outproj_even(pc, y_att, conv_w, w_out.astype(BF16), x, g, b, seq, ready[0]), ready[1:]


def _odd_mixer(x, w_in, b_i, b_f, norm_g, w_out, g, b, batch, seq, ffn):
    gate0 = 4 * MLSTM_W
    w_main = jnp.concatenate([w_in[:, :gate0], w_in[:, gate0 + GATE_COLS:]], axis=1).astype(BF16)
    w_gate = w_in[:, gate0:gate0 + GATE_COLS].astype(BF16)
    w_gate_pad = jnp.pad(w_gate, ((0, 0), (0, LANES - GATE_COLS)))
    p, gcol, grow = _proj_gates(x, w_main, w_gate_pad, w_gate.T)
    y_m = _mlstm(p, gcol, grow, b_i, b_f, norm_g, batch, seq)
    y_s = _stick_breaking(p, batch, seq, gate0)
    return _outproj_odd(y_m, y_s, w_out.astype(BF16), x, g, b, ffn)


def kernel(x, l0_ffn1_w_in, l0_ffn1_w_out, l0_ln1_g, l0_ln1_b, l0_mix_w_in, l0_conv_w, l0_mix_w_out, l0_ln2_g, l0_ln2_b, l0_ffn2_w_in, l0_ffn2_w_out, l0_ln3_g, l0_ln3_b, l1_ffn1_w_in, l1_ffn1_w_out, l1_ln1_g, l1_ln1_b, l1_mix_w_in, l1_mlstm_b_i, l1_mlstm_b_f, l1_mlstm_norm_g, l1_mix_w_out, l1_ln2_g, l1_ln2_b, l1_ffn2_w_in, l1_ffn2_w_out, l1_ln3_g, l1_ln3_b):
    batch, seq, d = x.shape
    t = x.reshape(batch * seq, d)
    t = _ffn_ln(t, (l0_ffn1_w_in.astype(BF16), l0_ffn1_w_out.astype(BF16), l0_ln1_g, l0_ln1_b))
    t, (l1_ffn1, l1_ffn2) = _even_mixer(
        t, l0_mix_w_in, l0_conv_w, l0_mix_w_out, l0_ln2_g, l0_ln2_b, batch, seq,
        (l0_ffn2_w_in, l0_ffn2_w_out, l0_ln3_g, l0_ln3_b),
        ((l1_ffn1_w_in, l1_ffn1_w_out, l1_ln1_g, l1_ln1_b), (l1_ffn2_w_in, l1_ffn2_w_out, l1_ln3_g, l1_ln3_b)))
    t = _ffn_ln(t, l1_ffn1)
    t = _odd_mixer(t, l1_mix_w_in, l1_mlstm_b_i, l1_mlstm_b_f, l1_mlstm_norm_g, l1_mix_w_out,
                   l1_ln2_g, l1_ln2_b, batch, seq, l1_ffn2)
    return t.reshape(batch, seq, d)
```

```python
import functools
import math

import jax
import jax.numpy as jnp
import numpy as np
from jax import lax
from jax.experimental import pallas as pl
from jax.experimental.pallas import tpu as pltpu

F32 = jnp.float32
BF16 = jnp.bfloat16

DEPTH = 2
ALPHA = (2 * DEPTH) ** 0.25
FFN_RES_WEIGHT = 0.5
LN_EPS = 1e-5

LANES = 128
MXU_TILE = 256
BF16_SUBLANES = 16
BLK = 128
CONV_CH = 256
CONV_WIDTH = 3
DSWA_HEAD_DIM = 64
DSWA_W = 768
DSWA_SPAN = 128
DSWA_PERIOD = 16
DSWA_SUPER = DSWA_PERIOD * BLK
D1_UNITS = 4
MLSTM_HEADS = 4
MLSTM_W = 512
MLSTM_HEAD_DIM = 128
MLSTM_CHUNK = 128
MLSTM_SEQS = 1
SB_HEAD_DIM = 64
SB_W = 512
GATE_COLS = 2 * MLSTM_HEADS

TOKEN_TILE = 512
FFN_TOKEN_TILE = 1024
SB_TILE = 128
SB_WINDOW = 256
SB_UNITS = 4
SB_SPENT_LOG2 = 127.0
LOG2E = math.log2(math.e)
NEG_BIG = -1e30
VMEM_LIMIT = 56 * 1024 * 1024


def _params(*sem):
    return pltpu.CompilerParams(dimension_semantics=sem, vmem_limit_bytes=VMEM_LIMIT)


def _layer_norm(z, g, b):
    mu = jnp.mean(z, axis=-1, keepdims=True)
    zc = z - mu
    var = jnp.mean(zc * zc, axis=-1, keepdims=True)
    return zc * lax.rsqrt(var + LN_EPS) * g + b


def _dot(a, b):
    return jnp.dot(a, b, preferred_element_type=F32)


def _dot_nt(a, b):
    return lax.dot_general(a, b, (((1,), (1,)), ((), ())), preferred_element_type=F32)


def _dot_tn(a, b):
    return lax.dot_general(a, b, (((0,), (0,)), ((), ())), preferred_element_type=F32)


def _split_dot(tri, x):
    hi = x.astype(BF16)
    lo = (x - hi.astype(F32)).astype(BF16)
    return _dot(jnp.concatenate([tri, tri], axis=1), jnp.concatenate([hi, lo], axis=0))


def _split_dot_r(x, tri):
    hi = x.astype(BF16)
    lo = (x - hi.astype(F32)).astype(BF16)
    return _dot(jnp.concatenate([hi, lo], axis=1), jnp.concatenate([tri, tri], axis=0))


def _neg_softplus(z):
    return -(jnp.maximum(z, 0.0) + jnp.log(1.0 + jnp.exp(-jnp.abs(z))))


def _softplus2(z2):
    return jnp.maximum(z2, 0.0) + jnp.log(1.0 + jnp.exp2(-jnp.abs(z2))) * LOG2E


def _scaled_q(q, scale):
    return (q.astype(F32) * scale).astype(BF16)


def _cast_blocks(rows, steps):
    return next(n for n in (64, 32, 16, 8, 4, 2, 1) if n <= steps and rows % (n * BF16_SUBLANES) == 0)


def _cast_specs(weights, step_of, steps):
    specs, shapes = [], []
    for w in weights:
        blocks = _cast_blocks(w.shape[0], steps)
        index = lambda *ids, blocks=blocks: (jnp.minimum(step_of(*ids), blocks - 1), 0)
        specs.append(pl.BlockSpec((w.shape[0] // blocks, w.shape[1]), index))
        shapes.append(jax.ShapeDtypeStruct(w.shape, BF16))
    return specs, shapes


def _cast_rows(src_refs, dst_refs):
    for src, dst in zip(src_refs, dst_refs):
        dst[...] = src[...].astype(BF16)


def _ffn_chunks(d_ff):
    tiles = d_ff // MXU_TILE
    cut = (tiles + 1) // 2 * MXU_TILE
    return ((0, cut), (cut, d_ff))


def _ffn_block(x, win_ref, wout_ref, g, b):
    d_ff = wout_ref.shape[0]
    halves = range(2)
    r = x.shape[0] // 2
    xs = [x[h * r:(h + 1) * r] for h in halves]
    xb = [t.astype(BF16) for t in xs]
    acc = [None, None]
    for lo, hi in _ffn_chunks(d_ff):
        gate_up = [(_dot(xb[h], win_ref[:, lo:hi]), _dot(xb[h], win_ref[:, d_ff + lo:d_ff + hi])) for h in halves]
        for h in halves:
            gate, up = gate_up[h]
            act = (gate * jax.nn.sigmoid(gate) * up).astype(BF16)
            part = _dot(act, wout_ref[lo:hi, :])
            acc[h] = part if acc[h] is None else acc[h] + part
    return jnp.concatenate([_layer_norm(ALPHA * xs[h] + FFN_RES_WEIGHT * acc[h], g, b) for h in halves], axis=0)


def _ffn_specs(d, d_ff):
    const = dict(pipeline_mode=pl.Buffered(1))
    return [
        pl.BlockSpec((d, 2 * d_ff), lambda i: (0, 0), **const),
        pl.BlockSpec((d_ff, d), lambda i: (0, 0), **const),
        pl.BlockSpec((1, d), lambda i: (0, 0)),
        pl.BlockSpec((1, d), lambda i: (0, 0)),
    ]


def _ffn_ln_body(x_ref, win_ref, wout_ref, g_ref, b_ref, o_ref):
    o_ref[...] = _ffn_block(x_ref[...], win_ref, wout_ref, g_ref[...], b_ref[...])


def _ffn_ln(x, ffn):
    n, d = x.shape
    w_in, w_out, g, b = ffn
    d_ff = w_out.shape[0]
    tm = FFN_TOKEN_TILE
    return pl.pallas_call(
        _ffn_ln_body,
        grid=(n // tm,),
        in_specs=[pl.BlockSpec((tm, d), lambda i: (i, 0))] + _ffn_specs(d, d_ff),
        out_specs=pl.BlockSpec((tm, d), lambda i: (i, 0)),
        out_shape=jax.ShapeDtypeStruct((n, d), F32),
        compiler_params=_params("parallel"),
        name="ffn_ln",
    )(x, w_in, w_out, g.reshape(1, d), b.reshape(1, d))


def _proj_even_body(*refs, tm, tn, n_cast):
    x_ref, perm_ref, w_ref = refs[:3]
    pc_ref, pq_ref = refs[3 + n_cast:5 + n_cast]
    _cast_rows(refs[3:3 + n_cast], refs[5 + n_cast:])
    cw = 3 * CONV_CH
    xb = x_ref[...].astype(BF16)
    pc_ref[...] = _dot(xb, w_ref[:, :cw]).astype(BF16)
    xg = _dot(perm_ref[...], xb).astype(BF16)
    rows = tm // DSWA_PERIOD
    for c in range((w_ref.shape[1] - cw) // tn):
        res = _dot(xg, w_ref[:, cw + c * tn:cw + (c + 1) * tn]).astype(BF16)
        pq_ref[0, :, :, c * tn:(c + 1) * tn] = res.reshape(DSWA_PERIOD, rows, tn)


def _proj_even(x, w, casts):
    n, d = x.shape
    cw = 3 * CONV_CH
    qw = w.shape[1] - cw
    tm = TOKEN_TILE
    tiles = DSWA_SUPER // tm
    rows = tm // DSWA_PERIOD
    dst = np.arange(tm)
    perm = np.zeros((tm, tm), np.float32)
    perm[dst, (dst % rows) * DSWA_PERIOD + dst // rows] = 1.0
    const = dict(pipeline_mode=pl.Buffered(1))
    cast_specs, cast_shapes = _cast_specs(casts, lambda i: i, n // tm)
    pc, pq, *cast = pl.pallas_call(
        functools.partial(_proj_even_body, tm=tm, tn=768, n_cast=len(casts)),
        grid=(n // tm,),
        in_specs=[
            pl.BlockSpec((tm, d), lambda i: (i, 0)),
            pl.BlockSpec((tm, tm), lambda i: (0, 0), **const),
            pl.BlockSpec((d, cw + qw), lambda i: (0, 0), **const),
        ] + cast_specs,
        out_specs=[
            pl.BlockSpec((tm, cw), lambda i: (i, 0)),
            pl.BlockSpec((1, DSWA_PERIOD, rows, qw), lambda i: (i // tiles, 0, i % tiles, 0)),
        ] + cast_specs,
        out_shape=[
            jax.ShapeDtypeStruct((n, cw), BF16),
            jax.ShapeDtypeStruct((n // DSWA_SUPER, DSWA_PERIOD, DSWA_SUPER // DSWA_PERIOD, qw), BF16),
        ] + cast_shapes,
        compiler_params=_params("arbitrary"),
        name="proj_even",
    )(x, jnp.asarray(perm, BF16), w, *casts)
    return pc, pq, cast


def _proj_gates_body(x_ref, w_ref, wg_ref, wgt_ref, o_ref, gc_ref, gr_ref, *, tn):
    xb = x_ref[...].astype(BF16)
    for c in range(w_ref.shape[1] // tn):
        o_ref[:, c * tn:(c + 1) * tn] = _dot(xb, w_ref[:, c * tn:(c + 1) * tn]).astype(BF16)
    gc_ref[...] = _dot(xb, wg_ref[...])
    gr_ref[...] = _dot_nt(wgt_ref[...], xb)


def _proj_gates(x, w, wg, wgt):
    n, d = x.shape
    p = w.shape[1]
    tm = TOKEN_TILE
    const = dict(pipeline_mode=pl.Buffered(1))
    return pl.pallas_call(
        functools.partial(_proj_gates_body, tn=512),
        grid=(n // tm,),
        in_specs=[
            pl.BlockSpec((tm, d), lambda i: (i, 0)),
            pl.BlockSpec((d, p), lambda i: (0, 0), **const),
            pl.BlockSpec((d, LANES), lambda i: (0, 0), **const),
            pl.BlockSpec((GATE_COLS, d), lambda i: (0, 0), **const),
        ],
        out_specs=[
            pl.BlockSpec((tm, p), lambda i: (i, 0)),
            pl.BlockSpec((tm, LANES), lambda i: (i, 0)),
            pl.BlockSpec((GATE_COLS, tm), lambda i: (0, i)),
        ],
        out_shape=[
            jax.ShapeDtypeStruct((n, p), BF16),
            jax.ShapeDtypeStruct((n, LANES), F32),
            jax.ShapeDtypeStruct((GATE_COLS, n), F32),
        ],
        compiler_params=_params("parallel"),
        name="proj_odd",
    )(x, w, wg, wgt)


def _dilated_biases():
    def make(q_pos, k_pos, k_is_prev):
        rel = q_pos[:, None] - k_pos[None, :]
        ok = (rel >= 0) & (rel <= DSWA_SPAN)
        variants = [ok, ok & ~k_is_prev[None, :]]
        return jnp.asarray(np.stack([np.where(v, 0.0, NEG_BIG) for v in variants]), F32)

    per = DSWA_PERIOD
    kj = np.arange(2 * BLK)
    b16 = make(np.arange(BLK), kj - BLK, kj < BLK)
    qa, qi = np.divmod(np.arange(BLK), BLK // 4)
    ka, ki = np.divmod(np.arange(2 * BLK), 2 * BLK // 4)
    b4 = make(4 * qi + qa, 4 * (ki - BLK // 4) + ka, ki < BLK // 4)
    qg, qi = np.divmod(np.arange(BLK), BLK // per)
    kg, ki = np.divmod(np.arange(2 * BLK), 2 * BLK // per)
    b1 = make(per * qi + qg, per * (ki - BLK // per) + kg, ki < BLK // per)
    return b16, b4, b1


def _dilated_body(b16_ref, b4_ref, b1_ref, q_ref, kp_ref, kc_ref, vp_ref, vc_ref, y_ref,
                  acc, m_s, l_s, nat, q32, k32, v32):
    per = DSWA_PERIOD
    first = jnp.where(pl.program_id(1) == 0, 1, 0)
    scale = LOG2E / math.sqrt(DSWA_HEAD_DIM)

    def window(prev_ref, cur_ref, groups, lo, hi):
        pieces = []
        for g in groups:
            if lo < BLK:
                pieces.append(prev_ref[0, g, lo:min(hi, BLK), :])
            if hi > BLK:
                pieces.append(cur_ref[0, g, max(lo, BLK) - BLK:hi - BLK, :])
        return jnp.concatenate(pieces, axis=0)

    def low_mask(rows):
        return lax.broadcasted_iota(jnp.int32, (rows, LANES), 1) < DSWA_HEAD_DIM

    def attend(units):
        scores = []
        for q2, kk, _, _ in units:
            low = low_mask(q2.shape[0])
            zero = jnp.zeros_like(q2)
            stacked = jnp.concatenate([jnp.where(low, q2, zero), jnp.where(low, zero, q2)], axis=0)
            scores.append(_dot_nt(stacked, kk))
        soft = []
        for s, (_, _, _, bias) in zip(scores, units):
            s = s + jnp.concatenate([bias, bias], axis=0)
            m = jnp.max(s, axis=-1, keepdims=True)
            p = jnp.exp2(s - m)
            soft.append((m, jnp.sum(p, axis=-1, keepdims=True), p.astype(BF16)))
        return [(m, l, _dot(p, vv)) for (m, l, p), (_, _, vv, _) in zip(soft, units)]

    def by_lane(x):
        r = x.shape[0] // 2
        return jnp.where(low_mask(r), x[:r], x[r:])

    def merge(part, m_old, l_old, acc_old):
        m, l, pv = (by_lane(x) for x in part)
        m_tot = jnp.maximum(m_old, m)
        a_old = jnp.exp2(m_old - m_tot)
        a_new = jnp.exp2(m - m_tot)
        return m_tot, l_old * a_old + l * a_new, acc_old * a_old + pv * a_new

    def gather(ref, lead, groups, rows):
        return jnp.concatenate([ref[lead + (g, rows, slice(None))] for g in groups], axis=0)

    def scatter(ref, groups, rows, value):
        n = value.shape[0] // len(groups)
        for j, g in enumerate(groups):
            ref[g, rows, :] = value[j * n:(j + 1) * n]

    state = (m_s, l_s, acc)

    def fold(parts, where):
        return [merge(part, *(gather(ref, (), groups, rows) for ref in state))
                for part, (groups, rows) in zip(parts, where)]

    def d16(it, carry):
        groups = [8 * it + j for j in range(8)]
        parts = attend([(_scaled_q(q_ref[0, g], scale), window(kp_ref, kc_ref, [g], 0, 2 * BLK),
                         window(vp_ref, vc_ref, [g], 0, 2 * BLK), b16_ref[first]) for g in groups])
        for g, part in zip(groups, parts):
            for ref, x in zip(state, part):
                ref[g] = by_lane(x)
        return carry

    lax.fori_loop(0, per // 8, d16, 0)

    def d4(r4, carry):
        nq = BLK // 4
        groups = [4 * a + r4 for a in range(4)]
        units, where = [], []
        for blk in range(4):
            qrows = slice(blk * nq, (blk + 1) * nq)
            lo, hi = BLK - nq + blk * nq, BLK + (blk + 1) * nq
            bias = b4_ref[first] if blk == 0 else b4_ref[0]
            units.append((_scaled_q(gather(q_ref, (0,), groups, qrows), scale),
                          window(kp_ref, kc_ref, groups, lo, hi), window(vp_ref, vc_ref, groups, lo, hi), bias))
            where.append((groups, qrows))
        for merged, (groups, rows) in zip(fold(attend(units), where), where):
            for ref, x in zip(state, merged):
                scatter(ref, groups, rows, x)
        return carry

    lax.fori_loop(0, 4, d4, 0)

    q32[...] = q_ref[0].astype(F32)
    k32[:, :BLK, :] = kp_ref[0].astype(F32)
    k32[:, BLK:, :] = kc_ref[0].astype(F32)
    v32[:, :BLK, :] = vp_ref[0].astype(F32)
    v32[:, BLK:, :] = vc_ref[0].astype(F32)

    def d1(it, carry):
        nq = BLK // per
        groups = list(range(per))
        units, where, blks = [], [], []
        for j in range(D1_UNITS):
            blk = D1_UNITS * it + j
            qrows = pl.ds(pl.multiple_of(blk * nq, nq), nq)
            krows = pl.ds(pl.multiple_of(BLK - nq + blk * nq, nq), 2 * nq)
            flag = jnp.where(blk == 0, first, 0)
            units.append((_scaled_q(gather(q32, (), groups, qrows), scale),
                          gather(k32, (), groups, krows).astype(BF16),
                          gather(v32, (), groups, krows).astype(BF16), b1_ref[flag]))
            where.append((groups, qrows))
            blks.append(blk)
        for j, (blk, (_, l, a)) in enumerate(zip(blks, fold(attend(units), where))):
            r = a.shape[0]
            out = a / l
            for g in groups:
                nat[j, pl.ds(g, nq, stride=per), :] = out[g * nq:(g + 1) * nq]
            y_ref[0, pl.ds(pl.multiple_of(blk * r, r), r), :] = nat[j].astype(BF16)
        return carry

    lax.fori_loop(0, DSWA_SUPER // (D1_UNITS * BLK), d1, 0)


def _dilated_attention(pq, b, s):
    per, w = DSWA_PERIOD, DSWA_W
    pairs = w // LANES
    nsb = s // DSWA_SUPER
    rows = DSWA_SUPER // per
    biases = _dilated_biases()
    blk = (1, per, rows, LANES)
    cur = lambda col: (lambda bi, sb, hp: (bi * nsb + sb, 0, 0, col * pairs + hp))
    prev = lambda col: (lambda bi, sb, hp: (bi * nsb + jnp.maximum(sb - 1, 0), 0, 0, col * pairs + hp))
    const = lambda a: pl.BlockSpec(a.shape, lambda bi, sb, hp: (0, 0, 0))
    return pl.pallas_call(
        _dilated_body,
        grid=(b, nsb, pairs),
        in_specs=[const(a) for a in biases] + [
            pl.BlockSpec(blk, cur(0)),
            pl.BlockSpec(blk, prev(1)),
            pl.BlockSpec(blk, cur(1)),
            pl.BlockSpec(blk, prev(2)),
            pl.BlockSpec(blk, cur(2)),
        ],
        out_specs=pl.BlockSpec((1, DSWA_SUPER, LANES), lambda bi, sb, hp: (bi * nsb + sb, 0, hp)),
        out_shape=jax.ShapeDtypeStruct((b * nsb, DSWA_SUPER, w), BF16),
        scratch_shapes=[
            pltpu.VMEM((per, rows, LANES), F32),
            pltpu.VMEM((per, rows, LANES), F32),
            pltpu.VMEM((per, rows, LANES), F32),
            pltpu.VMEM((D1_UNITS, BLK, LANES), F32),
            pltpu.VMEM((per, rows, LANES), F32),
            pltpu.VMEM((per, 2 * rows, LANES), F32),
            pltpu.VMEM((per, 2 * rows, LANES), F32),
        ],
        compiler_params=_params("parallel", "parallel", "parallel"),
        name="dilated",
    )(*biases, pq, pq, pq, pq, pq).reshape(b * s, w)


def _outproj_even_body(p_ref, halo_ref, y_ref, cw_ref, wo_ref, x_ref, g_ref, b_ref,
                       win_ref, wout_ref, g3_ref, b3_ref, o_ref, *, seq, tm, halo):
    c = CONV_CH
    pc = p_ref[...].astype(F32)
    bg, u = pc[:, :c], pc[:, c:2 * c] * pc[:, 2 * c:3 * c]
    ph = halo_ref[...].astype(F32)
    uh = ph[:, c:2 * c] * ph[:, 2 * c:3 * c]
    starts_sequence = (pl.program_id(0) * tm) % seq == 0
    uh = jnp.where(starts_sequence, 0.0, uh)
    full = jnp.concatenate([uh, u], axis=0)
    cw = cw_ref[...]
    conv = cw[2:3, :] * u + cw[1:2, :] * full[halo - 1:halo - 1 + tm] + cw[0:1, :] * full[halo - 2:halo - 2 + tm]
    y_conv = (bg * conv).astype(BF16)
    mix = _dot(y_conv, wo_ref[:c, :]) + _dot(y_ref[...], wo_ref[c:, :])
    mixed = _layer_norm(ALPHA * x_ref[...] + mix, g_ref[...], b_ref[...])
    o_ref[...] = _ffn_block(mixed, win_ref, wout_ref, g3_ref[...], b3_ref[...])


def _outproj_even(p, y_att, conv_w, w_out, x, g, b, seq, ffn):
    n, d = x.shape
    ffn_w_in, ffn_w_out, ffn_g, ffn_b = ffn
    tm = TOKEN_TILE
    halo = 16
    cw = 3 * CONV_CH
    return pl.pallas_call(
        functools.partial(_outproj_even_body, seq=seq, tm=tm, halo=halo),
        grid=(n // tm,),
        in_specs=[
            pl.BlockSpec((tm, cw), lambda i: (i, 0)),
            pl.BlockSpec((halo, cw), lambda i: (jnp.maximum(i * (tm // halo) - 1, 0), 0)),
            pl.BlockSpec((tm, DSWA_W), lambda i: (i, 0)),
            pl.BlockSpec((CONV_WIDTH, CONV_CH), lambda i: (0, 0)),
            pl.BlockSpec((d, d), lambda i: (0, 0), pipeline_mode=pl.Buffered(1)),
            pl.BlockSpec((tm, d), lambda i: (i, 0)),
            pl.BlockSpec((1, d), lambda i: (0, 0)),
            pl.BlockSpec((1, d), lambda i: (0, 0)),
        ] + _ffn_specs(d, ffn_w_out.shape[0]),
        out_specs=pl.BlockSpec((tm, d), lambda i: (i, 0)),
        out_shape=jax.ShapeDtypeStruct((n, d), F32),
        compiler_params=_params("parallel"),
        name="outproj_even_ffn",
    )(p, p, y_att, conv_w, w_out, x, g.reshape(1, d), b.reshape(1, d),
      ffn_w_in, ffn_w_out, ffn_g.reshape(1, d), ffn_b.reshape(1, d))


def _mlstm_body(*refs):
    L, dh, nh, ns = MLSTM_CHUNK, MLSTM_HEAD_DIM, MLSTM_HEADS, MLSTM_SEQS
    p_refs, gc_refs, gr_refs = refs[:ns], refs[ns:2 * ns], refs[2 * ns:3 * ns]
    bc_ref, br_ref, ng_ref, y_ref, c_scr, n_scr, m_scr = refs[3 * ns:]

    @pl.when(pl.program_id(1) == 0)
    def _():
        c_scr[...] = jnp.zeros_like(c_scr)
        n_scr[...] = jnp.zeros_like(n_scr)
        m_scr[...] = jnp.zeros_like(m_scr)

    row = lax.broadcasted_iota(jnp.int32, (L, L), 0)
    col = lax.broadcasted_iota(jnp.int32, (L, L), 1)
    causal = col <= row
    tri = causal.astype(BF16)
    tri_t = (row <= col).astype(BF16)

    scale = 1.0 / math.sqrt(dh)
    rep = lambda a: jnp.broadcast_to(a, (L, LANES))

    chains = [(j, h) for j in range(ns) for h in range(nh)]
    ids = range(len(chains))

    def part(i, w):
        j, h = chains[i]
        return p_refs[j][:, w * MLSTM_W + h * dh:w * MLSTM_W + (h + 1) * dh]

    q, k, v = ([part(i, w) for i in ids] for w in range(3))
    gcol = [gc_refs[j][...] + bc_ref[...] for j in range(ns)]
    grow = [gr_refs[j][...] + br_ref[...] for j in range(ns)]
    bcum_c = [_split_dot(tri, _neg_softplus(-g)) for g in gcol]
    bcum_r = [_split_dot_r(_neg_softplus(-g), tri_t) for g in grow]
    i_c = [rep(gcol[j][:, h:h + 1]) for j, h in chains]
    b_c = [rep(bcum_c[j][:, nh + h:nh + h + 1]) for j, h in chains]
    i_r = [grow[j][h:h + 1, :] for j, h in chains]
    b_r = [bcum_r[j][nh + h:nh + h + 1, :] for j, h in chains]
    m_prev = [m_scr[i] for i in ids]
    c_prev = [c_scr[i] for i in ids]
    n_prev = [n_scr[i] for i in ids]

    s = [_dot_nt(q[i], k[i]) for i in ids]
    q_c = [_dot(q[i], c_prev[i].astype(BF16)) for i in ids]

    for i in ids:
        b_tot = b_c[i][L - 1:L, :]
        d_state = b_tot - b_c[i] + i_c[i]
        m_new = jnp.maximum(b_tot + m_prev[i], jnp.max(d_state, axis=0, keepdims=True))
        kw = k[i].astype(F32) * (jnp.exp(d_state - m_new) * scale)
        decay = jnp.exp(b_tot + m_prev[i] - m_new)
        c_scr[i] = decay * c_prev[i] + _dot_tn(kw.astype(BF16), v[i])
        n_scr[i] = decay * n_prev[i] + jnp.sum(kw, axis=0, keepdims=True)
        m_scr[i] = m_new

    w_inter, m_t, qk = [], [], []
    for i in ids:
        d_intra = jnp.where(causal, b_c[i] - b_r[i] + i_r[i], NEG_BIG)
        d_inter = b_c[i] + m_prev[i]
        m_t.append(jnp.maximum(d_inter, jnp.max(d_intra, axis=-1, keepdims=True)))
        w_inter.append(jnp.exp(d_inter - m_t[i]))
        qk.append(s[i] * scale * jnp.exp(d_intra - m_t[i]))
    qk_v = [_dot(qk[i].astype(BF16), v[i]) for i in ids]

    for i, (j, h) in enumerate(chains):
        cols = slice(h * dh, (h + 1) * dh)
        num = w_inter[i] * q_c[i] + qk_v[i]
        den = (w_inter[i] * jnp.sum(q[i].astype(F32) * n_prev[i], axis=-1, keepdims=True)
               + jnp.sum(qk[i], axis=-1, keepdims=True))
        h_tilde = num / jnp.maximum(jnp.abs(den), jnp.exp(-m_t[i]))
        cell = jax.nn.sigmoid(part(i, 3).astype(F32)) * h_tilde
        mu = jnp.mean(cell, axis=-1, keepdims=True)
        cc = cell - mu
        var = jnp.mean(cc * cc, axis=-1, keepdims=True)
        y_ref[0, j, :, cols] = (cc * lax.rsqrt(var + LN_EPS) * ng_ref[:, cols]).astype(BF16)


def _mlstm(p, gcol, grow, b_i, b_f, norm_g, b, s):
    L, ns = MLSTM_CHUNK, MLSTM_SEQS
    nc = s // L
    chains = ns * MLSTM_HEADS
    bias = jnp.concatenate([b_i, b_f]).astype(F32)
    bias_c = jnp.zeros((1, LANES), F32).at[0, :GATE_COLS].set(bias)
    bias_r = bias.reshape(GATE_COLS, 1)
    tok = lambda j: (lambda bi, c: ((bi * ns + j) * nc + c, 0))
    tok_t = lambda j: (lambda bi, c: (0, (bi * ns + j) * nc + c))
    const = lambda bi, c: (0, 0)
    return pl.pallas_call(
        _mlstm_body,
        grid=(b // ns, nc),
        in_specs=(
            [pl.BlockSpec((L, 4 * MLSTM_W), tok(j)) for j in range(ns)]
            + [pl.BlockSpec((L, LANES), tok(j)) for j in range(ns)]
            + [pl.BlockSpec((GATE_COLS, L), tok_t(j)) for j in range(ns)]
            + [pl.BlockSpec((1, LANES), const), pl.BlockSpec((GATE_COLS, 1), const),
               pl.BlockSpec((1, MLSTM_W), const)]),
        out_specs=pl.BlockSpec((1, ns, L, MLSTM_W), lambda bi, c: (bi, 0, c, 0)),
        out_shape=jax.ShapeDtypeStruct((b // ns, ns, s, MLSTM_W), BF16),
        scratch_shapes=[
            pltpu.VMEM((chains, MLSTM_HEAD_DIM, MLSTM_HEAD_DIM), F32),
            pltpu.VMEM((chains, 1, MLSTM_HEAD_DIM), F32),
            pltpu.VMEM((chains, 1, LANES), F32),
        ],
        compiler_params=_params("parallel", "arbitrary"),
        name="mlstm",
    )(*([p] * ns + [gcol] * ns + [grow] * ns), bias_c, bias_r,
      norm_g.reshape(1, MLSTM_W)).reshape(b * s, MLSTM_W)


def _stick_body(bias_ref, q_ref, k_ref, v_ref, y_ref, acc_scr, c_scr):
    t, w = SB_TILE, SB_WINDOW
    units = range(SB_UNITS)
    starts =[(pl.program_id(2) * SB_UNITS + u) * t for u in units]

    def suffix(n):
        return (lax.broadcasted_iota(jnp.int32, (n, n), 0)
                >= lax.broadcasted_iota(jnp.int32, (n, n), 1)).astype(BF16)

    suffix_t, suffix_w = suffix(t), suffix(w)
    low = lax.broadcasted_iota(jnp.int32, (t, LANES), 1) < SB_HEAD_DIM

    def stacked(q2):
        zero = jnp.zeros_like(q2)
        return jnp.concatenate([jnp.where(low, q2, zero), jnp.where(low, zero, q2)], axis=0)

    def keys(first, n):
        return pl.ds(pl.multiple_of(first, t), n)

    def weights(z, incl, spent_later):
        return jnp.exp2(z - incl - spent_later).astype(BF16)

    def spent(sp):
        return jnp.sum(sp, axis=-1, keepdims=True)

    qs = [stacked(_scaled_q(q_ref[0, u * t:(u + 1) * t, :], LOG2E / math.sqrt(SB_HEAD_DIM))) for u in units]
    win = [jnp.maximum(s0 - w, 0) for s0 in starts]
    causal = bias_ref[...]
    causal = jnp.concatenate([causal, causal], axis=0)
    col_w = lax.broadcasted_iota(jnp.int32, (1, w), 1)
    earlier = [jnp.where(col_w < starts[u] - win[u], 0.0, NEG_BIG) for u in units]
    z_diag = [_dot_nt(qs[u], k_ref[0, keys(starts[u], t), :]) + causal for u in units]
    z_prev = [_dot_nt(qs[u], k_ref[0, keys(win[u], w), :]) + earlier[u] for u in units]
    sp_diag = [_softplus2(z) for z in z_diag]
    sp_prev = [_softplus2(z) for z in z_prev]
    in_diag = [_split_dot_r(sp, suffix_t) for sp in sp_diag]
    in_prev = [_split_dot_r(sp, suffix_w) for sp in sp_prev]
    c_diag = [spent(sp) for sp in sp_diag]
    a_diag = [weights(z_diag[u], in_diag[u], 0.0) for u in units]
    a_prev = [weights(z_prev[u], in_prev[u], c_diag[u]) for u in units]
    c_both = [c_diag[u] + spent(sp_prev[u]) for u in units]
    for u in units:
        acc_scr[u] = (_dot(a_diag[u], v_ref[0, keys(starts[u], t), :])
                      + _dot(a_prev[u], v_ref[0, keys(win[u], w), :]))
        c_scr[u] = jnp.broadcast_to(c_both[u], c_scr.shape[1:])

    def cond(carry):
        j, c_min = carry
        return (j >= 0) & (c_min < SB_SPENT_LOG2)

    for u in units:
        def body(carry, u=u):
            j, _ = carry
            c_old = c_scr[u]
            z = _dot_nt(qs[u], k_ref[0, keys(j * t, t), :])
            sp = _softplus2(z)
            a = weights(z, _split_dot_r(sp, suffix_t), jnp.concatenate([c_old] * (t // LANES), axis=1))
            acc_scr[u] += _dot(a, v_ref[0, keys(j * t, t), :])
            c_new = c_old + spent(sp)
            c_scr[u] = c_new
            return j - 1, jnp.min(c_new)

        lax.while_loop(cond, body, (win[u] // t - 1, jnp.min(c_both[u])))
        out = acc_scr[u]
        y_ref[0, u * t:(u + 1) * t, :] = jnp.where(low, out[:t], out[t:]).astype(BF16)


def _stick_breaking(p, b, s, col0):
    t = SB_TILE
    step = SB_UNITS * t
    pairs = SB_W // LANES
    pv = p.reshape(b, s, p.shape[1])
    c0 = col0 // LANES
    idx = np.arange(t)
    causal = jnp.asarray(np.where(idx[None, :] < idx[:, None], 0.0, NEG_BIG), F32)
    return pl.pallas_call(
        _stick_body,
        grid=(b, pairs, s // step),
        in_specs=[
            pl.BlockSpec((t, t), lambda bi, hp, qi: (0, 0)),
            pl.BlockSpec((1, step, LANES), lambda bi, hp, qi: (bi, qi, c0 + hp)),
            pl.BlockSpec((1, s, LANES), lambda bi, hp, qi: (bi, 0, c0 + pairs + hp)),
            pl.BlockSpec((1, s, LANES), lambda bi, hp, qi: (bi, 0, c0 + 2 * pairs + hp)),
        ],
        out_specs=pl.BlockSpec((1, step, LANES), lambda bi, hp, qi: (bi, qi, hp)),
        out_shape=jax.ShapeDtypeStruct((b, s, SB_W), BF16),
        scratch_shapes=[pltpu.VMEM((SB_UNITS, 2 * t, LANES), F32), pltpu.VMEM((SB_UNITS, 2 * t, LANES), F32)],
        compiler_params=_params("parallel", "parallel", "arbitrary"),
        name="stick_breaking",
    )(causal, pv, pv, pv).reshape(b * s, SB_W)


def _outproj_odd_body(ym_ref, ys_ref, wo_ref, x_ref, g_ref, b_ref, win_ref, wout_ref, g3_ref, b3_ref, o_ref):
    mix = _dot(ym_ref[...], wo_ref[:MLSTM_W, :]) + _dot(ys_ref[...], wo_ref[MLSTM_W:, :])
    mixed = _layer_norm(ALPHA * x_ref[...] + mix, g_ref[...], b_ref[...])
    o_ref[...] = _ffn_block(mixed, win_ref, wout_ref, g3_ref[...], b3_ref[...])


def _outproj_odd(y_m, y_s, w_out, x, g, b, ffn):
    n, d = x.shape
    ffn_w_in, ffn_w_out, ffn_g, ffn_b = ffn
    tm = TOKEN_TILE
    return pl.pallas_call(
        _outproj_odd_body,
        grid=(n // tm,),
        in_specs=[
            pl.BlockSpec((tm, MLSTM_W), lambda i: (i, 0)),
            pl.BlockSpec((tm, SB_W), lambda i: (i, 0)),
            pl.BlockSpec((d, d), lambda i: (0, 0), pipeline_mode=pl.Buffered(1)),
            pl.BlockSpec((tm, d), lambda i: (i, 0)),
            pl.BlockSpec((1, d), lambda i: (0, 0)),
            pl.BlockSpec((1, d), lambda i: (0, 0)),
        ] + _ffn_specs(d, ffn_w_out.shape[0]),
        out_specs=pl.BlockSpec((tm, d), lambda i: (i, 0)),
        out_shape=jax.ShapeDtypeStruct((n, d), F32),
        compiler_params=_params("parallel"),
        name="outproj_odd_ffn",
    )(y_m, y_s, w_out, x, g.reshape(1, d), b.reshape(1, d),
      ffn_w_in, ffn_w_out, ffn_g.reshape(1, d), ffn_b.reshape(1, d))


def _even_mixer(x, w_in, conv_w, w_out, g, b, batch, seq, ffn, later_ffns, later_w):
    all_ffns = (ffn,) + tuple(later_ffns)
    pc, pq, cast = _proj_even(x, w_in.astype(BF16), [w for f in all_ffns for w in f[:2]] + [w_out, later_w])
    ready = [tuple(cast[2 * i:2 * i + 2]) + f[2:] for i, f in enumerate(all_ffns)]
    w_out, later_w = cast[2 * len(all_ffns):]
    y_att = _dilated_attention(pq, batch, seq)
    return _outproj_even(pc, y_att, conv_w, w_out, x, g, b, seq, ready[0]), ready[1:], later_w


def _odd_mixer(x, w_in, b_i, b_f, norm_g, w_out, g, b, batch, seq, ffn):
    gate0 = 4 * MLSTM_W
    w_main = jnp.concatenate([w_in[:, :gate0], w_in[:, gate0 + GATE_COLS:]], axis=1).astype(BF16)
    w_gate = w_in[:, gate0:gate0 + GATE_COLS].astype(BF16)
    w_gate_pad = jnp.pad(w_gate, ((0, 0), (0, LANES - GATE_COLS)))
    p, gcol, grow = _proj_gates(x, w_main, w_gate_pad, w_gate.T)
    y_m = _mlstm(p, gcol, grow, b_i, b_f, norm_g, batch, seq)
    y_s = _stick_breaking(p, batch, seq, gate0)
    return _outproj_odd(y_m, y_s, w_out.astype(BF16), x, g, b, ffn)


def kernel(x, l0_ffn1_w_in, l0_ffn1_w_out, l0_ln1_g, l0_ln1_b, l0_mix_w_in, l0_conv_w, l0_mix_w_out, l0_ln2_g, l0_ln2_b, l0_ffn2_w_in, l0_ffn2_w_out, l0_ln3_g, l0_ln3_b, l1_ffn1_w_in, l1_ffn1_w_out, l1_ln1_g, l1_ln1_b, l1_mix_w_in, l1_mlstm_b_i, l1_mlstm_b_f, l1_mlstm_norm_g, l1_mix_w_out, l1_ln2_g, l1_ln2_b, l1_ffn2_w_in, l1_ffn2_w_out, l1_ln3_g, l1_ln3_b):
    batch, seq, d = x.shape
    t = x.reshape(batch * seq, d)
    t = _ffn_ln(t, (l0_ffn1_w_in.astype(BF16), l0_ffn1_w_out.astype(BF16), l0_ln1_g, l0_ln1_b))
    t, (l1_ffn1, l1_ffn2), l1_mix_w_out = _even_mixer(
        t, l0_mix_w_in, l0_conv_w, l0_mix_w_out, l0_ln2_g, l0_ln2_b, batch, seq,
        (l0_ffn2_w_in, l0_ffn2_w_out, l0_ln3_g, l0_ln3_b),
        ((l1_ffn1_w_in, l1_ffn1_w_out, l1_ln1_g, l1_ln1_b), (l1_ffn2_w_in, l1_ffn2_w_out, l1_ln3_g, l1_ln3_b)),
        l1_mix_w_out)
    t = _ffn_ln(t, l1_ffn1)
    t = _odd_mixer(t, l1_mix_w_in, l1_mlstm_b_i, l1_mlstm_b_f, l1_mlstm_norm_g, l1_mix_w_out,
                   l1_ln2_g, l1_ln2_b, batch, seq, l1_ffn2)
    return t.reshape(batch, seq, d)
```
